```python
import math
import jax, jax.numpy as jnp
from jax import lax
import numpy as np

D_MODEL = 1024
BATCH = 8
SEQ = 2048
DEPTH = 4
DEC_BATCH = 128
DEC_SEQ = 8
PAST_LEN = 16384
PAGE_SIZE = 128

W_GROUP = D_MODEL // 4
A_HEADS = 4
A_DK = W_GROUP // A_HEADS
A_DV = W_GROUP // A_HEADS
A_CONV = 4
A_CHUNK = 64
B_CH = W_GROUP
B_GROUPS = 4
B_CONV = 31
C_CH = W_GROUP
POOL_WINDOWS = (2, 4, 8, 16)
C_GROUP = C_CH // len(POOL_WINDOWS)
POOL_BUF = max(POOL_WINDOWS) - 1
D_CH = W_GROUP
D_BLOCKS = 4
D_BLOCK = D_CH // D_BLOCKS
D_CONV = 4
LRU_C = 8.0
A_QKV = A_HEADS * (2 * A_DK + A_DV)
OFF_ARAW = A_QKV
OFF_BETA = OFF_ARAW + A_HEADS
OFF_GATE = OFF_BETA + A_HEADS
OFF_B = OFF_GATE + A_HEADS * A_DV
OFF_C = OFF_B + 2 * B_CH
OFF_D = OFF_C + C_CH
IN_COLS = OFF_D + 2 * D_CH
MIX = A_HEADS * A_DV + B_CH + C_CH + D_CH
N_MEM = 256
X_HEADS = 4
X_HD = D_MODEL // X_HEADS
FFN = -(-8 * D_MODEL // (3 * 256)) * 256
EPS = 1e-6
F32 = jnp.float32

kernel_name = 'hymba_style_delta_conformer_pool_rglru_decoder_step'


def rmsnorm(x, g):
    xf = x.astype(F32)
    y = xf * lax.rsqrt(jnp.mean(xf * xf, axis=-1, keepdims=True) + EPS)
    return (y * g.astype(F32)).astype(x.dtype)


def l2norm(x):
    xf = x.astype(F32)
    return xf * lax.rsqrt(jnp.sum(xf * xf, axis=-1, keepdims=True) + EPS)


def causal_dwconv(buf, x, w):
    width = w.shape[0]
    xp = jnp.concatenate([buf.astype(x.dtype), x], axis=1)
    y = lax.conv_general_dilated(xp, w[:, None, :].astype(x.dtype), window_strides=(1,), padding='VALID',
                                 dimension_numbers=('NWC', 'WIO', 'NWC'), feature_group_count=x.shape[-1])
    return y, xp[:, xp.shape[1] - (width - 1):]


def gated_delta_chunked(q, k, v, g, beta, s0):
    Bn, T, H, DK = q.shape
    DV = v.shape[-1]
    n = -(-T // A_CHUNK)
    pad = n * A_CHUNK - T

    def chunk(a):
        a = jnp.pad(a.astype(F32), [(0, 0), (0, pad)] + [(0, 0)] * (a.ndim - 2))
        a = a.reshape((Bn, n, A_CHUNK) + a.shape[2:])
        return jnp.moveaxis(a, 3, 1)

    qc, kc, vc, gc, bc = (chunk(a) for a in (q, k, v, g, beta))
    gcum = jnp.cumsum(gc, axis=-1)
    idx = jnp.arange(A_CHUNK)
    causal = idx[:, None] >= idx[None, :]
    strict = idx[:, None] > idx[None, :]
    decay_mat = jnp.exp(jnp.where(causal, gcum[..., :, None] - gcum[..., None, :], -jnp.inf))
    kk = jnp.einsum('bhncd,bhnmd->bhncm', kc, kc)
    amat = jnp.where(strict, bc[..., None] * kk * decay_mat, 0.0)
    eye = jnp.eye(A_CHUNK, dtype=F32)
    rhs = jnp.concatenate([vc * bc[..., None], kc * (bc * jnp.exp(gcum))[..., None]], axis=-1)
    sol = lax.linalg.triangular_solve(eye + amat, rhs, left_side=True, lower=True, unit_diagonal=True)
    w_val, w_key = sol[..., :DV], sol[..., DV:]
    qk = jnp.einsum('bhncd,bhnmd->bhncm', qc, kc) * decay_mat

    def step(S, inp):
        qi, ki, wv, wk, qki, gi = inp
        u = wv - jnp.einsum('bhcd,bhde->bhce', wk, S)
        o = jnp.einsum('bhcd,bhde->bhce', qi * jnp.exp(gi)[..., None], S) + jnp.einsum('bhcm,bhme->bhce', qki, u)
        glast = gi[..., -1:]
        S = S * jnp.exp(glast)[..., None] + jnp.einsum('bhcd,bhce->bhde', ki * jnp.exp(glast - gi)[..., None], u)
        return S, o

    xs = tuple(jnp.moveaxis(a, 2, 0) for a in (qc, kc, w_val, w_key, qk, gcum))
    s_fin, o = lax.scan(step, s0.astype(F32), xs)
    o = jnp.moveaxis(o, 0, 2).reshape(Bn, H, n * A_CHUNK, DV)
    o = jnp.swapaxes(o, 1, 2)[:, :T]
    return o, s_fin.astype(s0.dtype)


def conformer_conv(u, buf, dw_w, dw_b, gn_g, gn_b, w_pw):
    glu = u[..., :B_CH] * jax.nn.sigmoid(u[..., B_CH:])
    y, new_buf = causal_dwconv(buf, glu, dw_w)
    y = (y + dw_b).astype(F32)
    yg = y.reshape(y.shape[:-1] + (B_GROUPS, B_CH // B_GROUPS))
    mu = jnp.mean(yg, axis=-1, keepdims=True)
    var = jnp.mean(jnp.square(yg - mu), axis=-1, keepdims=True)
    yn = ((yg - mu) * lax.rsqrt(var + EPS)).reshape(y.shape) * gn_g.astype(F32) + gn_b.astype(F32)
    return jax.nn.silu(yn).astype(u.dtype) @ w_pw, new_buf


def multiscale_pool(u, buf, pos0, w_pool, scale):
    T = u.shape[1]
    xp = jnp.concatenate([buf.astype(u.dtype), u], axis=1)
    cs = jnp.pad(jnp.cumsum(xp.astype(F32), axis=1), ((0, 0), (1, 0), (0, 0)))
    pos = pos0 + jnp.arange(T)
    outs = []
    for gi, w in enumerate(POOL_WINDOWS):
        sl = slice(gi * C_GROUP, (gi + 1) * C_GROUP)
        end = cs[:, POOL_BUF + 1:POOL_BUF + 1 + T, sl]
        start = cs[:, POOL_BUF + 1 - w:POOL_BUF + 1 - w + T, sl]
        cnt = jnp.minimum(pos + 1, w).astype(F32)[None, :, None]
        d = ((end - start) / cnt - u[..., sl].astype(F32)).astype(u.dtype)
        outs.append(d @ w_pool[gi])
    y = jnp.concatenate(outs, axis=-1) * scale
    return y, xp[:, xp.shape[1] - POOL_BUF:]


def rglru_block(u, buf, h0, conv_w, conv_b, w_rg, b_rg, w_ig, b_ig, lam):
    Bn, T, _ = u.shape
    gate = jax.nn.gelu(u[..., :D_CH].astype(F32))
    xr, new_buf = causal_dwconv(buf, u[..., D_CH:], conv_w)
    xr = (xr + conv_b).astype(F32)
    xb = xr.reshape(Bn, T, D_BLOCKS, D_BLOCK)
    r = jax.nn.sigmoid(jnp.einsum('btki,kij->btkj', xb, w_rg.astype(F32)).reshape(Bn, T, D_CH) + b_rg.astype(F32))
    i = jax.nn.sigmoid(jnp.einsum('btki,kij->btkj', xb, w_ig.astype(F32)).reshape(Bn, T, D_CH) + b_ig.astype(F32))
    log_a = -LRU_C * r * jax.nn.softplus(-lam.astype(F32))
    a = jnp.exp(log_a)
    b = jnp.sqrt(-jnp.expm1(2.0 * log_a)) * (i * xr)
    b = b.at[:, 0].add(a[:, 0] * h0.astype(F32))

    def combine(left, right):
        a_l, b_l = left
        a_r, b_r = right
        return a_l * a_r, a_r * b_l + b_r

    _, h = lax.associative_scan(combine, (a, b), axis=1)
    return (gate * h).astype(u.dtype), new_buf, h[:, -1].astype(h0.dtype)


def memory_kv(mem, g, w_kv):
    m = rmsnorm(mem, g) @ w_kv
    Bn = mem.shape[0]
    k = m[..., :X_HEADS * X_HD].reshape(Bn, -1, X_HEADS, X_HD)
    v = m[..., X_HEADS * X_HD:].reshape(Bn, -1, X_HEADS, X_HD)
    return k, v


def layer(x, mem_k, mem_v, state, pos0, p):
    dconv, dS, bconv, pbuf, lconv, lh = state
    Bn, T, _ = x.shape
    dt = x.dtype
    z = rmsnorm(x, p['norm_mix_pre']) @ p['w_in']
    y, dconv_new = causal_dwconv(dconv, z[..., :OFF_ARAW], p['conv_qkv'])
    y = jax.nn.silu(y)
    qa = l2norm(y[..., :A_HEADS * A_DK].reshape(Bn, T, A_HEADS, A_DK)) * (A_DK ** -0.5)
    ka = l2norm(y[..., A_HEADS * A_DK:2 * A_HEADS * A_DK].reshape(Bn, T, A_HEADS, A_DK))
    va = y[..., 2 * A_HEADS * A_DK:].reshape(Bn, T, A_HEADS, A_DV)
    g = -jnp.exp(p['a_log'].astype(F32)) * jax.nn.softplus(z[..., OFF_ARAW:OFF_BETA].astype(F32) + p['dt_bias'].astype(F32))
    beta = jax.nn.sigmoid(z[..., OFF_BETA:OFF_GATE].astype(F32))
    oa, dS_new = gated_delta_chunked(qa, ka, va, g, beta, dS)
    oa = rmsnorm(oa.astype(dt), p['onorm_a']) * jax.nn.silu(z[..., OFF_GATE:OFF_B].reshape(Bn, T, A_HEADS, A_DV))
    oa = oa.reshape(Bn, T, A_HEADS * A_DV)
    ob, bconv_new = conformer_conv(z[..., OFF_B:OFF_C], bconv, p['dw_b'], p['dwbias_b'], p['gn_gain_b'], p['gn_bias_b'], p['w_pw_b'])
    oc, pbuf_new = multiscale_pool(z[..., OFF_C:OFF_D], pbuf, pos0, p['w_pool'], p['scale_pool'])
    od, lconv_new, lh_new = rglru_block(z[..., OFF_D:IN_COLS], lconv, lh, p['conv_d'], p['conv_bias_d'],
                                        p['w_rg'], p['b_rg'], p['w_ig'], p['b_ig'], p['lam_d'])
    mix = jnp.concatenate([oa, ob, oc, od], axis=-1)
    x = x + rmsnorm(mix @ p['w_out'], p['norm_mix_post'])
    h = rmsnorm(x, p['norm_x_pre'])
    q = (h @ p['w_xq']).reshape(Bn, T, X_HEADS, X_HD)
    s = jnp.einsum('bthd,bmhd->bhtm', q, mem_k.astype(dt)).astype(F32) * (X_HD ** -0.5)
    pr = jax.nn.softmax(s, axis=-1).astype(dt)
    o = jnp.einsum('bhtm,bmhd->bthd', pr, mem_v.astype(dt)).reshape(Bn, T, X_HEADS * X_HD)
    x = x + rmsnorm(o @ p['w_xo'], p['norm_x_post'])
    h = rmsnorm(x, p['norm_ffn_pre'])
    gu = h @ p['w_ffn_in']
    ff = (jax.nn.silu(gu[..., :FFN]) * gu[..., FFN:]) @ p['w_ffn_out']
    x = x + rmsnorm(ff, p['norm_ffn_post'])
    return x, (dconv_new, dS_new, bconv_new, pbuf_new, lconv_new, lh_new)


def setup_inputs(seed: int = 0) -> dict:
    key = jax.random.key(seed)
    ks = iter(jax.random.split(key, 64))
    L = DEPTH

    def nrm(shape, s):
        return jax.random.normal(next(ks), shape, F32) * s

    def gain(n):
        return 1.0 + nrm((L, n), 0.02)

    inp = {}
    inp['x_prompt'] = nrm((BATCH, SEQ, D_MODEL), 1.0)
    inp['x_sample'] = nrm((DEC_BATCH, DEC_SEQ, D_MODEL), 1.0)
    inp['mem_prompt'] = nrm((BATCH, N_MEM, D_MODEL), 1.0)
    inp['state_delta'] = nrm((L, DEC_BATCH, A_HEADS, A_DK, A_DV), A_DK ** -0.5)
    inp['state_delta_conv'] = nrm((L, DEC_BATCH, A_CONV - 1, A_QKV), 1.0)
    inp['state_conf_conv'] = nrm((L, DEC_BATCH, B_CONV - 1, B_CH), 0.5)
    inp['state_pool'] = nrm((L, DEC_BATCH, POOL_BUF, C_CH), 1.0)
    inp['state_lru_conv'] = nrm((L, DEC_BATCH, D_CONV - 1, D_CH), 1.0)
    inp['state_lru_h'] = nrm((L, DEC_BATCH, D_CH), 0.5)
    inp['cache_mem_k'] = nrm((L, DEC_BATCH, N_MEM, X_HEADS, X_HD), 1.0)
    inp['cache_mem_v'] = nrm((L, DEC_BATCH, N_MEM, X_HEADS, X_HD), 1.0)
    inp['norm_mix_pre'] = gain(D_MODEL)
    inp['norm_mix_post'] = gain(D_MODEL)
    inp['w_in'] = nrm((L, D_MODEL, IN_COLS), D_MODEL ** -0.5)
    inp['conv_qkv'] = nrm((L, A_CONV, A_QKV), A_CONV ** -0.5)
    inp['a_log'] = jnp.log(jax.random.uniform(next(ks), (L, A_HEADS), F32, 1.0, 16.0))
    dtv = jnp.exp(jax.random.uniform(next(ks), (L, A_HEADS), F32, math.log(1e-3), math.log(1e-1)))
    inp['dt_bias'] = dtv + jnp.log(-jnp.expm1(-dtv))
    inp['onorm_a'] = gain(A_DV)
    inp['dw_b'] = nrm((L, B_CONV, B_CH), B_CONV ** -0.5)
    inp['dwbias_b'] = nrm((L, B_CH), 0.02)
    inp['gn_gain_b'] = gain(B_CH)
    inp['gn_bias_b'] = nrm((L, B_CH), 0.02)
    inp['w_pw_b'] = nrm((L, B_CH, B_CH), B_CH ** -0.5)
    inp['w_pool'] = nrm((L, len(POOL_WINDOWS), C_GROUP, C_GROUP), C_GROUP ** -0.5)
    inp['scale_pool'] = 1.0 + nrm((L, C_CH), 0.1)
    inp['conv_d'] = nrm((L, D_CONV, D_CH), D_CONV ** -0.5)
    inp['conv_bias_d'] = nrm((L, D_CH), 0.02)
    inp['w_rg'] = nrm((L, D_BLOCKS, D_BLOCK, D_BLOCK), D_BLOCK ** -0.5)
    inp['b_rg'] = nrm((L, D_CH), 0.02)
    inp['w_ig'] = nrm((L, D_BLOCKS, D_BLOCK, D_BLOCK), D_BLOCK ** -0.5)
    inp['b_ig'] = nrm((L, D_CH), 0.02)
    a0 = jax.random.uniform(next(ks), (L, D_CH), F32, 0.9, 0.999)
    inp['lam_d'] = jnp.log(a0) - jnp.log1p(-a0)
    inp['w_out'] = nrm((L, MIX, D_MODEL), MIX ** -0.5)
    inp['norm_x_pre'] = gain(D_MODEL)
    inp['norm_x_post'] = gain(D_MODEL)
    inp['norm_mem'] = gain(D_MODEL)
    inp['w_xq'] = nrm((L, D_MODEL, X_HEADS * X_HD), D_MODEL ** -0.5)
    inp['w_xkv'] = nrm((L, D_MODEL, 2 * X_HEADS * X_HD), D_MODEL ** -0.5)
    inp['w_xo'] = nrm((L, X_HEADS * X_HD, D_MODEL), (X_HEADS * X_HD) ** -0.5)
    inp['norm_ffn_pre'] = gain(D_MODEL)
    inp['norm_ffn_post'] = gain(D_MODEL)
    inp['w_ffn_in'] = nrm((L, D_MODEL, 2 * FFN), D_MODEL ** -0.5)
    inp['w_ffn_out'] = nrm((L, FFN, D_MODEL), FFN ** -0.5)
    return inp


def reference(x_prompt, x_sample, mem_prompt, state_delta, state_delta_conv, state_conf_conv, state_pool,
              state_lru_conv, state_lru_h, cache_mem_k, cache_mem_v, norm_mix_pre, norm_mix_post, w_in, conv_qkv,
              a_log, dt_bias, onorm_a, dw_b, dwbias_b, gn_gain_b, gn_bias_b, w_pw_b, w_pool, scale_pool, conv_d,
              conv_bias_d, w_rg, b_rg, w_ig, b_ig, lam_d, w_out, norm_x_pre, norm_x_post, norm_mem, w_xq, w_xkv,
              w_xo, norm_ffn_pre, norm_ffn_post, w_ffn_in, w_ffn_out):
    dt = x_prompt.dtype
    bp = x_prompt.shape[0]
    xp, xs = x_prompt, x_sample
    new_p = [[] for _ in range(6)]
    new_s = [[] for _ in range(6)]
    mem_k_list, mem_v_list = [], []
    for l in range(DEPTH):
        p = {'norm_mix_pre': norm_mix_pre[l], 'norm_mix_post': norm_mix_post[l], 'w_in': w_in[l],
             'conv_qkv': conv_qkv[l], 'a_log': a_log[l], 'dt_bias': dt_bias[l], 'onorm_a': onorm_a[l],
             'dw_b': dw_b[l], 'dwbias_b': dwbias_b[l], 'gn_gain_b': gn_gain_b[l], 'gn_bias_b': gn_bias_b[l],
             'w_pw_b': w_pw_b[l], 'w_pool': w_pool[l], 'scale_pool': scale_pool[l], 'conv_d': conv_d[l],
             'conv_bias_d': conv_bias_d[l], 'w_rg': w_rg[l], 'b_rg': b_rg[l], 'w_ig': w_ig[l], 'b_ig': b_ig[l],
             'lam_d': lam_d[l], 'w_out': w_out[l], 'norm_x_pre': norm_x_pre[l], 'norm_x_post': norm_x_post[l],
             'w_xq': w_xq[l], 'w_xo': w_xo[l], 'norm_ffn_pre': norm_ffn_pre[l],
             'norm_ffn_post': norm_ffn_post[l], 'w_ffn_in': w_ffn_in[l], 'w_ffn_out': w_ffn_out[l]}
        mk_p, mv_p = memory_kv(mem_prompt, norm_mem[l], w_xkv[l])
        zero_state = (jnp.zeros((bp, A_CONV - 1, A_QKV), dt),
                      jnp.zeros((bp, A_HEADS, A_DK, A_DV), dt),
                      jnp.zeros((bp, B_CONV - 1, B_CH), dt),
                      jnp.zeros((bp, POOL_BUF, C_CH), dt),
                      jnp.zeros((bp, D_CONV - 1, D_CH), dt),
                      jnp.zeros((bp, D_CH), dt))
        xp, st_p = layer(xp, mk_p, mv_p, zero_state, 0, p)
        st_in = (state_delta_conv[l], state_delta[l], state_conf_conv[l], state_pool[l],
                 state_lru_conv[l], state_lru_h[l])
        xs, st_s = layer(xs, cache_mem_k[l], cache_mem_v[l], st_in, PAST_LEN, p)
        for j in range(6):
            new_p[j].append(st_p[j])
            new_s[j].append(st_s[j])
        mem_k_list.append(mk_p)
        mem_v_list.append(mv_p)
    dconv_p, delta_p, conf_p, pool_p, lconv_p, lh_p = (jnp.stack(a) for a in new_p)
    dconv_s, delta_s, conf_s, pool_s, lconv_s, lh_s = (jnp.stack(a) for a in new_s)
    mem_k_p = jnp.stack(mem_k_list)
    mem_v_p = jnp.stack(mem_v_list)
    return (xp, xs, delta_p, delta_s, dconv_p, dconv_s, conf_p, conf_s, pool_p, pool_s,
            lconv_p, lconv_s, lh_p, lh_s, mem_k_p, mem_v_p)
```

```python
import functools
import math

import jax
import jax.numpy as jnp
from jax import lax
from jax.experimental import pallas as pl
from jax.experimental.pallas import tpu as pltpu

F32 = jnp.float32
BF16 = jnp.bfloat16
EPS = 1e-6

D_MODEL = 1024
W_GROUP = 256
A_HEADS = 4
A_DK = 64
A_CONV = 4
A_QKV = 768
B_CONV = 31
B_GROUPS = 4
POOL_WINDOWS = (2, 4, 8, 16)
POOL_BUF = 15
D_CONV = 4
LRU_C = 8.0
N_MEM = 256
X_HEADS = 4
X_HD = 256
FFN = 2816
PAST_LEN = 16384

Z_QKV = 0
Z_GATE = 768
Z_B = 1024
Z_C = 1536
Z_D = 1792
Z_AUX = 2304
Z_COLS = 2432

STACK = A_HEADS * 64
VMEM_LIMIT = 56 * 1024 * 1024


def _cparams(*sem):
    return pltpu.CompilerParams(dimension_semantics=sem, vmem_limit_bytes=VMEM_LIMIT)


def _const_spec(shape):
    nd = len(shape)
    return pl.BlockSpec(shape, lambda *_: (0,) * nd, pipeline_mode=pl.Buffered(1))


def _rms(x, g):
    return x * lax.rsqrt(jnp.mean(x * x, axis=-1, keepdims=True) + EPS) * g


def _sigmoid(x):
    return 1.0 / (1.0 + jnp.exp(-x))


def _silu(x):
    return x * _sigmoid(x)


def _softplus(x):
    return jnp.maximum(x, 0.0) + jnp.log1p(jnp.exp(-jnp.abs(x)))


def _dot(a, b):
    return jnp.dot(a.astype(BF16), b.astype(BF16), preferred_element_type=F32)


def _dot_nt(a, b):
    return lax.dot_general(a.astype(BF16), b.astype(BF16), (((1,), (1,)), ((), ())),
                           preferred_element_type=F32)


def _dot_2pass(x, w):
    hi = x.astype(BF16)
    lo = (x - hi.astype(F32)).astype(BF16)
    return (jnp.dot(hi, w, preferred_element_type=F32) + jnp.dot(lo, w, preferred_element_type=F32))


def _norm_matmul_kernel(x_ref, g_ref, w_ref, o_ref):
    h = _rms(x_ref[...], g_ref[...])
    o_ref[...] = _dot(h, w_ref[...]).astype(o_ref.dtype)


def _norm_matmul(x, g, w, out_dtype, tm):
    m, d = x.shape
    n = w.shape[1]
    return pl.pallas_call(
        _norm_matmul_kernel,
        grid=(m // tm,),
        in_specs=[pl.BlockSpec((tm, d), lambda i: (i, 0)), _const_spec((1, d)), _const_spec((d, n))],
        out_specs=pl.BlockSpec((tm, n), lambda i: (i, 0)),
        out_shape=jax.ShapeDtypeStruct((m, n), out_dtype),
        compiler_params=_cparams("parallel"),
        name="norm_matmul",
    )(x, g, w)


def _seg_scan_sum(x, seg_len, row_in_seg):
    s = 1
    while s < seg_len:
        x = x + jnp.where(row_in_seg >= s, pltpu.roll(x, s, axis=0), 0.0)
        s *= 2
    return x


def _seg_scan_linear(a, b, seg_len, row_in_seg):
    s = 1
    while s < seg_len:
        keep = row_in_seg >= s
        a_sh = jnp.where(keep, pltpu.roll(a, s, axis=0), 1.0)
        b_sh = jnp.where(keep, pltpu.roll(b, s, axis=0), 0.0)
        b = a * b_sh + b
        a = a * a_sh
        s *= 2
    return a, b


def _conv_segment(scr, seg, x, w_rows, width, off, tn, row_block):
    scr[seg, off:off + tn, :] = x
    outs = []
    for r0 in range(0, tn, row_block):
        rb = min(row_block, tn - r0)
        acc = None
        for j in range(width):
            start = off - (width - 1) + j + r0
            term = scr[seg, start:start + rb, :] * w_rows[j]
            acc = term if acc is None else acc + term
        outs.append(acc)
    tail = scr[seg, off + tn - (width - 1):off + tn, :]
    scr[seg, off - (width - 1):off, :] = tail
    return outs[0] if len(outs) == 1 else jnp.concatenate(outs, axis=0)


def _pool_segment(scr, seg, u, off, tn):
    scr[seg, off:off + tn, :] = u

    def back(j):
        return scr[seg, off - j:off - j + tn, :]

    s2 = u + back(1)
    s4 = s2 + back(2) + back(3)
    s8 = s4
    for j in range(4, 8):
        s8 = s8 + back(j)
    s16 = s8
    for j in range(8, 16):
        s16 = s16 + back(j)
    lane = lax.broadcasted_iota(jnp.int32, (1, W_GROUP), 1)
    wsum = jnp.where(lane < 64, s2, jnp.where(lane < 128, s4, jnp.where(lane < 192, s8, s16)))
    tail = scr[seg, off + tn - POOL_BUF:off + tn, :]
    scr[seg, off - POOL_BUF:off, :] = tail
    return wsum


def _mixer_kernel(z_ref, dconv_ref, ds_ref, bconv_ref, pbuf_ref, lconv_ref, lh_ref,
                  wqkv_ref, hp_ref, onorm_ref, dw_ref, brow_ref, wpw_ref, wpool_ref, convd_ref,
                  wrg_ref, wig_ref, gavg_ref,
                  mix_ref, dconv_o, ds_o, bconv_o, pbuf_o, lconv_o, lh_o,
                  xq_scr, xb_scr, xc_scr, xd_scr, s_scr, h_scr,
                  *, nseg, tn, chunk, pos0):
    t = pl.program_id(1)
    nt = pl.num_programs(1)
    rows = nseg * tn
    nchunks = rows // 64
    groups = STACK // chunk
    seq_per_chunk = 64 // chunk
    qoff, boff, coff, doff = 8, 32, 16, 8

    @pl.when(t == 0)
    def _load_state():
        for s in range(nseg):
            xq_scr[s, qoff - (A_CONV - 1):qoff, :] = dconv_ref[s]
            xb_scr[s, boff - (B_CONV - 1):boff, :] = bconv_ref[s]
            xc_scr[s, coff - POOL_BUF:coff, :] = pbuf_ref[s]
            xd_scr[s, doff - (D_CONV - 1):doff, :] = lconv_ref[s]
            h_scr[s] = lh_ref[s]
            for h in range(A_HEADS):
                g = h * seq_per_chunk + s
                s_scr[g * 64:(g + 1) * 64, :] = ds_ref[s, h]

    def zcols(lo, hi):
        if nseg == 1:
            return z_ref[0, :, lo:hi]
        return z_ref[:, :, lo:hi].reshape(rows, hi - lo)

    def per_segment(fn, x):
        outs = [fn(s, x[s * tn:(s + 1) * tn]) for s in range(nseg)]
        return outs[0] if nseg == 1 else jnp.concatenate(outs, axis=0)

    row = lax.broadcasted_iota(jnp.int32, (rows, 1), 0)
    row_in_seg = row % tn
    brow = brow_ref[...]

    ub = zcols(Z_B, Z_C)
    glu = ub[:, :W_GROUP] * _sigmoid(ub[:, W_GROUP:])
    dw_rows = [dw_ref[j:j + 1, :] for j in range(B_CONV)]
    yb = per_segment(lambda s, x: _conv_segment(xb_scr, s, x, dw_rows, B_CONV, boff, tn, 32), glu)
    yb = yb + brow[0:1]
    gavg = gavg_ref[...]
    mu = _dot_2pass(yb, gavg)
    yc = yb - mu
    var = _dot_2pass(yc * yc, gavg)
    yn = yc * lax.rsqrt(var + EPS) * brow[1:2] + brow[2:3]
    ob = _dot(_silu(yn), wpw_ref[...])

    uc = zcols(Z_C, Z_D)
    wsum = per_segment(lambda s, x: _pool_segment(xc_scr, s, x, coff, tn), uc)
    lane = lax.broadcasted_iota(jnp.int32, (1, W_GROUP), 1)
    wlen = jnp.where(lane < 64, 2.0, jnp.where(lane < 128, 4.0, jnp.where(lane < 192, 8.0, 16.0)))
    pos = (pos0 + t * tn + row_in_seg).astype(F32)
    cnt = jnp.minimum(pos + 1.0, wlen)
    oc = _dot(wsum / cnt - uc, wpool_ref[...]) * brow[3:4]

    ud = zcols(Z_D, Z_AUX)
    gate_d = jax.nn.gelu(ud[:, :W_GROUP])
    cd_rows = [convd_ref[j:j + 1, :] for j in range(D_CONV)]
    xr = per_segment(lambda s, x: _conv_segment(xd_scr, s, x, cd_rows, D_CONV, doff, tn, 64), ud[:, W_GROUP:])
    xr = xr + brow[4:5]
    r_gate = _sigmoid(_dot(xr, wrg_ref[...]) + brow[5:6])
    i_gate = _sigmoid(_dot(xr, wig_ref[...]) + brow[6:7])
    log_a = -LRU_C * r_gate * _softplus(-brow[7:8])
    a = jnp.exp(log_a)
    b = jnp.sqrt(-jnp.tanh(log_a) * (a * a + 1.0)) * (i_gate * xr)
    a_cum, h_zero = _seg_scan_linear(a, b, tn, row_in_seg)
    h0 = per_segment(lambda s, x: jnp.broadcast_to(h_scr[s], (tn, W_GROUP)), a)
    hseq = a_cum * h0 + h_zero
    od = gate_d * hseq
    for s in range(nseg):
        h_scr[s] = hseq[(s + 1) * tn - 1:(s + 1) * tn, :]

    wq_rows = [wqkv_ref[j:j + 1, :] for j in range(A_CONV)]
    yq = per_segment(lambda s, x: _conv_segment(xq_scr, s, x, wq_rows, A_CONV, qoff, tn, 64), zcols(Z_QKV, Z_GATE))
    yq = _silu(yq)
    aux = zcols(Z_AUX, Z_COLS)
    g_slab = -jnp.exp(hp_ref[0:1, :]) * _softplus(aux + hp_ref[1:2, :])
    beta_slab = _sigmoid(aux)
    gate_a = _silu(zcols(Z_GATE, Z_B))
    onorm = onorm_ref[...]

    ri = lax.broadcasted_iota(jnp.int32, (STACK, STACK), 0)
    ci = lax.broadcasted_iota(jnp.int32, (STACK, STACK), 1)
    same = (ri // chunk) == (ci // chunk)
    causal = same & (ri >= ci)
    strict = same & (ri > ci)
    eye = (ri == ci).astype(F32)
    last_col = ci == (ri // chunk) * chunk + (chunk - 1)
    crow = lax.broadcasted_iota(jnp.int32, (64, 1), 0) % chunk
    nrep = groups * 64 // STACK
    erow = lax.broadcasted_iota(jnp.int32, (STACK, groups * 64), 0)
    ecol = lax.broadcasted_iota(jnp.int32, (STACK, groups * 64), 1)
    expand_mask = (erow // chunk) == (ecol // 64)
    trow = lax.broadcasted_iota(jnp.int32, (groups * 64, STACK), 0)
    tcol = lax.broadcasted_iota(jnp.int32, (groups * 64, STACK), 1)
    expand_mask_t = (trow // 64) == (tcol // chunk)
    last_col_t = tcol == (trow // 64) * chunk + (chunk - 1)
    eye64 = (lax.broadcasted_iota(jnp.int32, (64, 64), 0)
             == lax.broadcasted_iota(jnp.int32, (64, 64), 1)).astype(BF16)

    def stack_heads(x, base):
        return jnp.concatenate([x[:, base + h * 64:base + (h + 1) * 64] for h in range(A_HEADS)], axis=0)

    def stack_cols(x, base):
        return jnp.concatenate([x[:, base + h:base + h + 1] for h in range(A_HEADS)], axis=0)

    oa_chunks = []
    for c in range(nchunks):
        sl = slice(c * 64, (c + 1) * 64)
        yq_c = yq[sl]
        q = stack_heads(yq_c, 0)
        k = stack_heads(yq_c, 256)
        v = stack_heads(yq_c, 512)
        q = q * lax.rsqrt(jnp.sum(q * q, axis=-1, keepdims=True) + EPS) * (A_DK ** -0.5)
        k = k * lax.rsqrt(jnp.sum(k * k, axis=-1, keepdims=True) + EPS)
        gcs = _seg_scan_sum(g_slab[sl], chunk, crow)
        g_col = stack_cols(gcs, 0)
        beta_col = stack_cols(beta_slab[sl], A_HEADS)
        g_c = jnp.broadcast_to(g_col, (STACK, STACK))
        g_r = g_c.T
        decay = jnp.exp(jnp.where(causal, g_c - g_r, -1e30))
        g_last = jnp.sum(jnp.where(last_col, g_r, 0.0), axis=1, keepdims=True)
        kk = _dot_nt(k, k)
        amat = jnp.where(strict, beta_col * kk * decay, 0.0)
        qk = _dot_nt(q, k) * decay
        x_inv = eye - jnp.where((ri // 2) == (ci // 2), amat, 0.0)
        blk = 2
        while blk < chunk:
            m_l = jnp.where(((ri // (2 * blk)) == (ci // (2 * blk))) & ((ri // blk) != (ci // blk)), amat, 0.0)
            x_inv = x_inv - _dot(_dot(x_inv, m_l), x_inv)
            blk *= 2
        rhs = jnp.concatenate([v * beta_col, k * (beta_col * jnp.exp(g_col))], axis=1)
        sol = _dot(x_inv, rhs)
        w_val, w_key = sol[:, :64], sol[:, 64:]
        s_old = s_scr[...]
        s_b = s_old.astype(BF16)

        def expand(x):
            return jnp.where(expand_mask, jnp.concatenate([x] * groups, axis=1), 0.0)

        u = w_val - _dot(expand(w_key), s_b)
        o = _dot(expand(q * jnp.exp(g_col)), s_b) + _dot(qk, u)
        kd = k * jnp.exp(g_last - g_col)
        kd_t = lax.dot_general(eye64, kd.astype(BF16), (((1,), (1,)), ((), ())), preferred_element_type=F32)
        u_t = lax.dot_general(eye64, u.astype(BF16), (((1,), (1,)), ((), ())), preferred_element_type=F32)
        kd_t_exp = jnp.where(expand_mask_t, jnp.concatenate([kd_t] * groups, axis=0), 0.0)
        g_r_rep = g_r if nrep == 1 else jnp.concatenate([g_r] * nrep, axis=0)
        s_decay = jnp.exp(jnp.sum(jnp.where(last_col_t, g_r_rep, 0.0), axis=1, keepdims=True))
        s_scr[...] = s_old * s_decay + _dot_nt(kd_t_exp, u_t)
        o = o * lax.rsqrt(jnp.mean(o * o, axis=-1, keepdims=True) + EPS) * onorm
        oa_chunks.append(jnp.concatenate([o[h * 64:(h + 1) * 64] for h in range(A_HEADS)], axis=1))
    oa = oa_chunks[0] if nchunks == 1 else jnp.concatenate(oa_chunks, axis=0)
    oa = oa * gate_a

    mix = jnp.concatenate([oa, ob, oc, od], axis=1).astype(mix_ref.dtype)
    if nseg == 1:
        mix_ref[0] = mix
    else:
        mix_ref[...] = mix.reshape(nseg, tn, 4 * W_GROUP)

    @pl.when(t == nt - 1)
    def _store_state():
        for s in range(nseg):
            dconv_o[s] = xq_scr[s, qoff - (A_CONV - 1):qoff, :]
            bconv_o[s] = xb_scr[s, boff - (B_CONV - 1):boff, :]
            pbuf_o[s] = xc_scr[s, coff - POOL_BUF:coff, :]
            lconv_o[s] = xd_scr[s, doff - (D_CONV - 1):doff, :]
            lh_o[s] = h_scr[s]
            for h in range(A_HEADS):
                g = h * seq_per_chunk + s
                ds_o[s, h] = s_scr[g * 64:(g + 1) * 64, :]


def _mixer(z, state, lw, *, nseg, tn, chunk, pos0):
    bsz, tlen, _ = z.shape
    dconv, ds, bconv, pbuf, lconv, lh = state
    rows = nseg * tn
    assert rows % 64 == 0 and (nseg == 1 or rows == 64) and tlen % tn == 0 and bsz % nseg == 0
    assert (nseg == 1 and chunk == 64) or (nseg > 1 and chunk == tn and tlen == tn)
    groups = STACK // chunk
    grid = (bsz // nseg, tlen // tn)

    def state_spec(a):
        blk = (nseg,) + a.shape[1:]
        nd = a.ndim
        return pl.BlockSpec(blk, lambda b, t: (b,) + (0,) * (nd - 1))

    weights = (lw["conv_qkv"], lw["hp"], lw["onorm"], lw["dw_b"], lw["brow"], lw["w_pw"], lw["w_pool"],
               lw["conv_d"], lw["w_rg"], lw["w_ig"], lw["gavg"])
    states = (dconv, ds, bconv, pbuf, lconv, lh)
    out_shape = (jax.ShapeDtypeStruct((bsz, tlen, 4 * W_GROUP), F32),) + tuple(
        jax.ShapeDtypeStruct(a.shape, a.dtype) for a in states)
    kern = functools.partial(_mixer_kernel, nseg=nseg, tn=tn, chunk=chunk, pos0=pos0)
    return pl.pallas_call(
        kern,
        grid=grid,
        in_specs=[pl.BlockSpec((nseg, tn, Z_COLS), lambda b, t: (b, t, 0))]
        + [state_spec(a) for a in states] + [_const_spec(w.shape) for w in weights],
        out_specs=[pl.BlockSpec((nseg, tn, 4 * W_GROUP), lambda b, t: (b, t, 0))]
        + [state_spec(a) for a in states],
        out_shape=out_shape,
        scratch_shapes=[
            pltpu.VMEM((nseg, 8 + tn, A_QKV), F32),
            pltpu.VMEM((nseg, 32 + tn, W_GROUP), F32),
            pltpu.VMEM((nseg, 16 + tn, W_GROUP), F32),
            pltpu.VMEM((nseg, 8 + tn, W_GROUP), F32),
            pltpu.VMEM((groups * 64, 64), F32),
            pltpu.VMEM((nseg, 1, W_GROUP), F32),
        ],
        compiler_params=_cparams("parallel", "arbitrary"),
        name="mixer",
    )(z, *states, *weights)


def _post_mixer_kernel(mix_ref, x_ref, wout_ref, gpost_ref, gpre_ref, wq_ref, x1_ref, q_ref):
    x1 = x_ref[...] + _rms(_dot(mix_ref[...], wout_ref[...]), gpost_ref[...])
    x1_ref[...] = x1
    q_ref[...] = _dot(_rms(x1, gpre_ref[...]), wq_ref[...]).astype(q_ref.dtype)


def _post_mixer(mix, x, lw, tm):
    m, d = x.shape
    row = lambda i: (i, 0)
    return pl.pallas_call(
        _post_mixer_kernel,
        grid=(m // tm,),
        in_specs=[pl.BlockSpec((tm, d), row), pl.BlockSpec((tm, d), row), _const_spec((d, d)),
                  _const_spec((1, d)), _const_spec((1, d)), _const_spec((d, d))],
        out_specs=[pl.BlockSpec((tm, d), row), pl.BlockSpec((tm, d), row)],
        out_shape=(jax.ShapeDtypeStruct((m, d), F32), jax.ShapeDtypeStruct((m, d), BF16)),
        compiler_params=_cparams("parallel"),
        name="post_mixer",
    )(mix, x, lw["w_out"], lw["norm_mix_post"], lw["norm_x_pre"], lw["w_xq"])


def _attn_kernel(q_ref, k_ref, v_ref, x_ref, wo_ref, g_ref, o_ref):
    q = q_ref[0]
    outs = []
    for h in range(X_HEADS):
        sl = slice(h * X_HD, (h + 1) * X_HD)
        s = _dot_nt(q[:, sl], k_ref[0, :, sl]) * (X_HD ** -0.5)
        e = jnp.exp(s - jnp.max(s, axis=-1, keepdims=True))
        p = e / jnp.sum(e, axis=-1, keepdims=True)
        outs.append(_dot(p, v_ref[0, :, sl]))
    o = jnp.concatenate(outs, axis=1)
    o_ref[0] = x_ref[0] + _rms(_dot(o, wo_ref[...]), g_ref[...])


def _attn(q, k, v, x, lw, tq):
    bsz, tlen, d = x.shape
    tile = lambda b, t: (b, t, 0)
    mem = lambda b, t: (b, 0, 0)
    return pl.pallas_call(
        _attn_kernel,
        grid=(bsz, tlen // tq),
        in_specs=[pl.BlockSpec((1, tq, d), tile), pl.BlockSpec((1, N_MEM, d), mem),
                  pl.BlockSpec((1, N_MEM, d), mem), pl.BlockSpec((1, tq, d), tile),
                  _const_spec((d, d)), _const_spec((1, d))],
        out_specs=pl.BlockSpec((1, tq, d), tile),
        out_shape=jax.ShapeDtypeStruct((bsz, tlen, d), F32),
        compiler_params=_cparams("parallel", "arbitrary"),
        name="attn",
    )(q, k, v, x, lw["w_xo"], lw["norm_x_post"])


def _ffn_kernel(x_ref, gpre_ref, win_ref, wout_ref, gpost_ref, o_ref):
    x = x_ref[...]
    gu = _dot(_rms(x, gpre_ref[...]), win_ref[...])
    act = _silu(gu[:, :FFN]) * gu[:, FFN:]
    o_ref[...] = x + _rms(_dot(act, wout_ref[...]), gpost_ref[...])


def _ffn(x, lw, tm):
    m, d = x.shape
    row = lambda i: (i, 0)
    return pl.pallas_call(
        _ffn_kernel,
        grid=(m // tm,),
        in_specs=[pl.BlockSpec((tm, d), row), _const_spec((1, d)), _const_spec((d, 2 * FFN)),
                  _const_spec((FFN, d)), _const_spec((1, d))],
        out_specs=pl.BlockSpec((tm, d), row),
        out_shape=jax.ShapeDtypeStruct((m, d), F32),
        compiler_params=_cparams("parallel"),
        name="ffn",
    )(x, lw["norm_ffn_pre"], lw["w_ffn_in"], lw["w_ffn_out"], lw["norm_ffn_post"])


def _block_diag(w):
    n, k, _ = w.shape
    out = jnp.zeros((n * k, n * k), w.dtype)
    for i in range(n):
        out = out.at[i * k:(i + 1) * k, i * k:(i + 1) * k].set(w[i])
    return out


def _prep_layer(p):
    w_in = p["w_in"]
    off_araw, off_gate = A_QKV, A_QKV + 2 * A_HEADS
    w_in = jnp.concatenate(
        [w_in[:, :off_araw], w_in[:, off_gate:], w_in[:, off_araw:off_gate],
         jnp.zeros((D_MODEL, Z_COLS - w_in.shape[1]), w_in.dtype)], axis=1)
    pad4 = lambda a: jnp.pad(a, (0, 128 - a.shape[0]))[None, :]
    row = lambda a: a[None, :]
    lw = {
        "w_in": w_in.astype(BF16),
        "norm_mix_pre": row(p["norm_mix_pre"]),
        "conv_qkv": p["conv_qkv"],
        "hp": jnp.concatenate([pad4(p["a_log"]), pad4(p["dt_bias"])], axis=0),
        "onorm": row(p["onorm_a"]),
        "dw_b": p["dw_b"],
        "brow": jnp.stack([p["dwbias_b"], p["gn_gain_b"], p["gn_bias_b"], p["scale_pool"],
                           p["conv_bias_d"], p["b_rg"], p["b_ig"], p["lam_d"]], axis=0),
        "w_pw": p["w_pw_b"].astype(BF16),
        "w_pool": _block_diag(p["w_pool"]).astype(BF16),
        "conv_d": p["conv_d"],
        "w_rg": _block_diag(p["w_rg"]).astype(BF16),
        "w_ig": _block_diag(p["w_ig"]).astype(BF16),
        "gavg": _block_diag(jnp.full((B_GROUPS, 64, 64), 1.0 / 64, F32)).astype(BF16),
        "w_out": p["w_out"].astype(BF16),
        "norm_mix_post": row(p["norm_mix_post"]),
        "norm_x_pre": row(p["norm_x_pre"]),
        "w_xq": p["w_xq"].astype(BF16),
        "w_xo": p["w_xo"].astype(BF16),
        "norm_x_post": row(p["norm_x_post"]),
        "norm_ffn_pre": row(p["norm_ffn_pre"]),
        "w_ffn_in": p["w_ffn_in"].astype(BF16),
        "w_ffn_out": p["w_ffn_out"].astype(BF16),
        "norm_ffn_post": row(p["norm_ffn_post"]),
    }
    return lw


def _layer(x, mem_k, mem_v, state, lw, *, nseg, tn, chunk, pos0, tm, tq):
    bsz, tlen, d = x.shape
    xf = x.reshape(bsz * tlen, d)
    z = _norm_matmul(xf, lw["norm_mix_pre"], lw["w_in"], F32, tm).reshape(bsz, tlen, Z_COLS)
    mix, *new_state = _mixer(z, state, lw, nseg=nseg, tn=tn, chunk=chunk, pos0=pos0)
    x1, q = _post_mixer(mix.reshape(bsz * tlen, d), xf, lw, tm)
    x2 = _attn(q.reshape(bsz, tlen, d), mem_k, mem_v, x1.reshape(bsz, tlen, d), lw, tq)
    x3 = _ffn(x2.reshape(bsz * tlen, d), lw, tm)
    return x3.reshape(bsz, tlen, d), new_state


def kernel(x_prompt, x_sample, mem_prompt, state_delta, state_delta_conv, state_conf_conv, state_pool,
           state_lru_conv, state_lru_h, cache_mem_k, cache_mem_v, norm_mix_pre, norm_mix_post, w_in, conv_qkv,
           a_log, dt_bias, onorm_a, dw_b, dwbias_b, gn_gain_b, gn_bias_b, w_pw_b, w_pool, scale_pool, conv_d,
           conv_bias_d, w_rg, b_rg, w_ig, b_ig, lam_d, w_out, norm_x_pre, norm_x_post, norm_mem, w_xq, w_xkv,
           w_xo, norm_ffn_pre, norm_ffn_post, w_ffn_in, w_ffn_out):
    depth = w_in.shape[0]
    bp, tp, d = x_prompt.shape
    bs, ts, _ = x_sample.shape
    dt = x_prompt.dtype
    per_layer = dict(norm_mix_pre=norm_mix_pre, norm_mix_post=norm_mix_post, w_in=w_in, conv_qkv=conv_qkv,
                     a_log=a_log, dt_bias=dt_bias, onorm_a=onorm_a, dw_b=dw_b, dwbias_b=dwbias_b,
                     gn_gain_b=gn_gain_b, gn_bias_b=gn_bias_b, w_pw_b=w_pw_b, w_pool=w_pool,
                     scale_pool=scale_pool, conv_d=conv_d, conv_bias_d=conv_bias_d, w_rg=w_rg, b_rg=b_rg,
                     w_ig=w_ig, b_ig=b_ig, lam_d=lam_d, w_out=w_out, norm_x_pre=norm_x_pre,
                     norm_x_post=norm_x_post, w_xq=w_xq, w_xo=w_xo, norm_ffn_pre=norm_ffn_pre,
                     norm_ffn_post=norm_ffn_post, w_ffn_in=w_ffn_in, w_ffn_out=w_ffn_out)
    mem_flat = mem_prompt.reshape(bp * N_MEM, d)
    zero_state = (jnp.zeros((bp, A_CONV - 1, A_QKV), dt), jnp.zeros((bp, A_HEADS, A_DK, A_DK), dt),
                  jnp.zeros((bp, B_CONV - 1, W_GROUP), dt), jnp.zeros((bp, POOL_BUF, W_GROUP), dt),
                  jnp.zeros((bp, D_CONV - 1, W_GROUP), dt), jnp.zeros((bp, 1, W_GROUP), dt))
    xp, xs = x_prompt, x_sample
    new_p = [[] for _ in range(6)]
    new_s = [[] for _ in range(6)]
    mem_k_list, mem_v_list = [], []
    tn_p = min(tp, 256)
    for l in range(depth):
        lw = _prep_layer({k: v[l] for k, v in per_layer.items()})
        g_mem = norm_mem[l][None, :]
        w_kv = w_xkv[l].astype(BF16)
        mk = _norm_matmul(mem_flat, g_mem, w_kv[:, :d], F32, 256).reshape(bp, N_MEM, d)
        mv = _norm_matmul(mem_flat, g_mem, w_kv[:, d:], F32, 256).reshape(bp, N_MEM, d)
        xp, st_p = _layer(xp, mk, mv, zero_state, lw, nseg=1, tn=tn_p, chunk=64, pos0=0, tm=256, tq=256)
        st_in = (state_delta_conv[l], state_delta[l], state_conf_conv[l], state_pool[l], state_lru_conv[l],
                 state_lru_h[l][:, None, :])
        xs, st_s = _layer(xs, cache_mem_k[l].reshape(bs, N_MEM, d), cache_mem_v[l].reshape(bs, N_MEM, d),
                          st_in, lw, nseg=64 // ts, tn=ts, chunk=ts, pos0=PAST_LEN, tm=256, tq=ts)
        for j in range(6):
            new_p[j].append(st_p[j])
            new_s[j].append(st_s[j])
        mem_k_list.append(mk.reshape(bp, N_MEM, X_HEADS, X_HD))
        mem_v_list.append(mv.reshape(bp, N_MEM, X_HEADS, X_HD))
    dconv_p, delta_p, conf_p, pool_p, lconv_p, lh_p = (jnp.stack(a) for a in new_p)
    dconv_s, delta_s, conf_s, pool_s, lconv_s, lh_s = (jnp.stack(a) for a in new_s)
    return (xp, xs, delta_p, delta_s, dconv_p, dconv_s, conf_p, conf_s, pool_p, pool_s,
            lconv_p, lconv_s, lh_p[:, :, 0, :], lh_s[:, :, 0, :], jnp.stack(mem_k_list), jnp.stack(mem_v_list))
```

```python
import functools

import jax
import jax.numpy as jnp
from jax import lax
from jax.experimental import pallas as pl
from jax.experimental.pallas import tpu as pltpu

F32 = jnp.float32
BF16 = jnp.bfloat16
EPS = 1e-6

D_MODEL = 1024
W_GROUP = 256
A_HEADS = 4
A_DK = 64
A_CONV = 4
A_QKV = 768
B_CONV = 31
B_GROUPS = 4
POOL_BUF = 15
D_CONV = 4
LRU_C = 8.0
N_MEM = 256
X_HEADS = 4
X_HD = 256
FFN = 2816
PAST_LEN = 16384

Z_QKV = 0
Z_GATE = 768
Z_B = 1024
Z_C = 1536
Z_D = 1792
Z_G = 2304
Z_BETA = 2560
Z_COLS = 2816

SUBLANES = 8
STACK = A_HEADS * 64
NEG = -1e30
VMEM_LIMIT = 56 * 1024 * 1024


def _cparams(*sem):
    return pltpu.CompilerParams(dimension_semantics=sem, vmem_limit_bytes=VMEM_LIMIT)


def _const_spec(shape):
    nd = len(shape)
    return pl.BlockSpec(shape, lambda *_: (0,) * nd, pipeline_mode=pl.Buffered(1))


def _rms(x, g):
    return x * lax.rsqrt(jnp.mean(x * x, axis=-1, keepdims=True) + EPS) * g


def _sigmoid(x):
    return 0.5 * jnp.tanh(0.5 * x) + 0.5


def _silu(x):
    return x * _sigmoid(x)


def _softplus(x):
    return jnp.maximum(x, 0.0) + jnp.log(1.0 + jnp.exp(-jnp.abs(x)))


def _dot(a, b):
    return jnp.dot(a.astype(BF16), b.astype(BF16), preferred_element_type=F32)


def _dot_nt(a, b):
    return lax.dot_general(a.astype(BF16), b.astype(BF16), (((1,), (1,)), ((), ())),
                           preferred_element_type=F32)


def _dot_2pass(x, w):
    hi = x.astype(BF16)
    lo = (x - hi.astype(F32)).astype(BF16)
    return (jnp.dot(hi, w, preferred_element_type=F32) + jnp.dot(lo, w, preferred_element_type=F32))


def _norm_matmul_kernel(x_ref, g_ref, w_ref, o_ref):
    h = _rms(x_ref[...], g_ref[...])
    o_ref[...] = _dot(h, w_ref[...]).astype(o_ref.dtype)


def _norm_matmul(x, g, w, out_dtype, tm):
    m, d = x.shape
    n = w.shape[1]
    return pl.pallas_call(
        _norm_matmul_kernel,
        grid=(m // tm,),
        in_specs=[pl.BlockSpec((tm, d), lambda i: (i, 0)), _const_spec((1, d)), _const_spec((d, n))],
        out_specs=pl.BlockSpec((tm, n), lambda i: (i, 0)),
        out_shape=jax.ShapeDtypeStruct((m, n), out_dtype),
        compiler_params=_cparams("parallel"),
        name="norm_matmul",
    )(x, g, w)


def _seg_scan_sum(x, seg_len, row_in_seg):
    s = 1
    while s < seg_len:
        x = x + jnp.where(row_in_seg >= s, pltpu.roll(x, s, axis=0), 0.0)
        s *= 2
    return x


def _linear_scan(a, b, carries, seg_rows):
    rows, width = a.shape
    nblk = rows // SUBLANES
    a = a.reshape(nblk, SUBLANES, width)
    b = b.reshape(nblk, SUBLANES, width)
    sub = lax.broadcasted_iota(jnp.int32, (1, SUBLANES, 1), 1)
    s = 1
    while s < SUBLANES:
        keep = sub >= s
        a_sh = jnp.where(keep, pltpu.roll(a, s, axis=1), 1.0)
        b_sh = jnp.where(keep, pltpu.roll(b, s, axis=1), 0.0)
        b = a * b_sh + b
        a = a * a_sh
        s *= 2
    blk_per_seg = seg_rows // SUBLANES
    outs = []
    carry = None
    for g in range(nblk):
        if g % blk_per_seg == 0:
            carry = carries[g // blk_per_seg]
        hg = a[g] * carry + b[g]
        carry = hg[SUBLANES - 1:SUBLANES]
        outs.append(hg)
    return jnp.concatenate(outs, axis=0)


def _conv_segment(scr, seg, x, w_rows, width, off, tn, row_block):
    scr[seg, off:off + tn, :] = x
    base = off - (width - 1)
    outs = []
    for r0 in range(0, tn, row_block):
        rb = min(row_block, tn - r0)
        acc = None
        for res in range(SUBLANES):
            part = None
            for p in range(base, base + width):
                if p % SUBLANES != res:
                    continue
                lo = p - res + r0
                term = scr[seg, lo:lo + rb + SUBLANES, :] * w_rows[p - base]
                part = term if part is None else part + term
            if part is None:
                continue
            piece = part[res:res + rb]
            acc = piece if acc is None else acc + piece
        outs.append(acc)
    tail = scr[seg, off + tn - (width - 1):off + tn, :]
    scr[seg, off - (width - 1):off, :] = tail
    return outs[0] if len(outs) == 1 else jnp.concatenate(outs, axis=0)


def _pool_segment(scr, seg, u, off, tn):
    scr[seg, off:off + tn, :] = u
    x = scr[seg, 0:off + tn, :]
    s2 = x + pltpu.roll(x, 1, axis=0)
    s4 = s2 + pltpu.roll(s2, 2, axis=0)
    s8 = s4 + pltpu.roll(s4, 4, axis=0)
    s16 = s8[off:] + s8[off - 8:off - 8 + tn]
    lane = lax.broadcasted_iota(jnp.int32, (1, W_GROUP), 1)
    wsum = jnp.where(lane < 64, s2[off:], jnp.where(lane < 128, s4[off:], jnp.where(lane < 192, s8[off:], s16)))
    tail = scr[seg, off + tn - POOL_BUF:off + tn, :]
    scr[seg, off - POOL_BUF:off, :] = tail
    return wsum


def _mixer_kernel(x_ref, dconv_ref, ds_ref, bconv_ref, pbuf_ref, lconv_ref, lh_ref,
                  gin_ref, win_ref, wqkv_ref, hp_ref, onorm_ref, dw_ref, brow_ref, wpw_ref, wpool_ref, convd_ref,
                  wrg_ref, wig_ref, gavg_ref,
                  mix_ref, dconv_o, ds_o, bconv_o, pbuf_o, lconv_o, lh_o,
                  xq_scr, xb_scr, xc_scr, xd_scr, s_scr, h_scr,
                  *, nseg, tn, chunk, pos0):
    t = pl.program_id(1)
    nt = pl.num_programs(1)
    rows = nseg * tn
    nchunks = rows // 64
    groups = STACK // chunk
    seq_per_chunk = 64 // chunk
    qoff, boff, coff, doff = 8, 32, 16, 8

    @pl.when(t == 0)
    def _load_state():
        for s in range(nseg):
            xq_scr[s, 0:qoff, :] = jnp.zeros((qoff, A_QKV), F32)
            xq_scr[s, qoff + tn:, :] = jnp.zeros((SUBLANES, A_QKV), F32)
            xb_scr[s, 0:SUBLANES, :] = jnp.zeros((SUBLANES, W_GROUP), F32)
            xb_scr[s, boff + tn:, :] = jnp.zeros((SUBLANES, W_GROUP), F32)
            xc_scr[s, 0:SUBLANES, :] = jnp.zeros((SUBLANES, W_GROUP), F32)
            xd_scr[s, 0:doff, :] = jnp.zeros((doff, W_GROUP), F32)
            xd_scr[s, doff + tn:, :] = jnp.zeros((SUBLANES, W_GROUP), F32)
            xq_scr[s, qoff - (A_CONV - 1):qoff, :] = dconv_ref[s]
            xb_scr[s, boff - (B_CONV - 1):boff, :] = bconv_ref[s]
            xc_scr[s, coff - POOL_BUF:coff, :] = pbuf_ref[s]
            xd_scr[s, doff - (D_CONV - 1):doff, :] = lconv_ref[s]
            h_scr[s] = lh_ref[s]
            for h in range(A_HEADS):
                g = h * seq_per_chunk + s
                s_scr[g * 64:(g + 1) * 64, :] = ds_ref[s, h]

    x_in = x_ref[0] if nseg == 1 else x_ref[...].reshape(rows, D_MODEL)
    z = _dot(_rms(x_in, gin_ref[...]), win_ref[...])

    def per_segment(fn, x):
        outs = [fn(s, x[s * tn:(s + 1) * tn]) for s in range(nseg)]
        return outs[0] if nseg == 1 else jnp.concatenate(outs, axis=0)

    row = lax.broadcasted_iota(jnp.int32, (rows, 1), 0)
    row_in_seg = row % tn
    brow = brow_ref[...]
    gavg = gavg_ref[...]
    results = {}

    def section_b_conv():
        ub = z[:, Z_B:Z_C]
        glu = ub[:, :W_GROUP] * _sigmoid(ub[:, W_GROUP:])
        dw_rows = [dw_ref[j:j + 1, :] for j in range(B_CONV)]
        yb = per_segment(lambda s, x: _conv_segment(xb_scr, s, x, dw_rows, B_CONV, boff, tn, 64), glu)
        results["yb"] = yb + brow[0:1]

    def section_b_norm():
        yb = results["yb"]
        yc = yb - _dot_2pass(yb, gavg)
        var = _dot(yc * yc, gavg)
        yn = yc * lax.rsqrt(var + EPS) * brow[1:2] + brow[2:3]
        results["ob"] = _dot(_silu(yn), wpw_ref[...])

    def section_c():
        uc = z[:, Z_C:Z_D]
        wsum = per_segment(lambda s, x: _pool_segment(xc_scr, s, x, coff, tn), uc)
        lane = lax.broadcasted_iota(jnp.int32, (1, W_GROUP), 1)
        wlen = jnp.where(lane < 64, 2.0, jnp.where(lane < 128, 4.0, jnp.where(lane < 192, 8.0, 16.0)))
        pos = (pos0 + t * tn + row_in_seg).astype(F32)
        cnt = jnp.minimum(pos + 1.0, wlen)
        results["oc"] = _dot(wsum / cnt - uc, wpool_ref[...]) * brow[3:4]

    def section_d_gates():
        ud = z[:, Z_D:Z_G]
        results["gate_d"] = jax.nn.gelu(ud[:, :W_GROUP])
        cd_rows = [convd_ref[j:j + 1, :] for j in range(D_CONV)]
        xr = per_segment(lambda s, x: _conv_segment(xd_scr, s, x, cd_rows, D_CONV, doff, tn, 64), ud[:, W_GROUP:])
        xr = xr + brow[4:5]
        r_gate = _sigmoid(_dot(xr, wrg_ref[...]) + brow[5:6])
        i_gate = _sigmoid(_dot(xr, wig_ref[...]) + brow[6:7])
        log_a = -LRU_C * r_gate * _softplus(-brow[7:8])
        a = jnp.exp(log_a)
        results["lru_a"] = a
        results["lru_b"] = jnp.sqrt(-jnp.tanh(log_a) * (a * a + 1.0)) * (i_gate * xr)

    def section_d_scan():
        hseq = _linear_scan(results["lru_a"], results["lru_b"], [h_scr[s] for s in range(nseg)], tn)
        results["od"] = results["gate_d"] * hseq
        for s in range(nseg):
            h_scr[s] = hseq[(s + 1) * tn - 1:(s + 1) * tn, :]

    side_work = [section_b_conv, section_b_norm, section_c, section_d_gates, section_d_scan]

    wq_rows = [wqkv_ref[j:j + 1, :] for j in range(A_CONV)]
    yq = per_segment(lambda s, x: _conv_segment(xq_scr, s, x, wq_rows, A_CONV, qoff, tn, 64), z[:, Z_QKV:Z_GATE])
    yq = _silu(yq)
    hp = hp_ref[...]
    g_log = -jnp.exp(hp[0:1]) * _softplus(z[:, Z_G:Z_BETA] + hp[1:2])
    beta = _sigmoid(z[:, Z_BETA:Z_COLS])
    g_cum = _seg_scan_sum(g_log, chunk, row % chunk)
    g_end = jnp.concatenate(
        [jnp.broadcast_to(g_cum[(i + 1) * chunk - 1:(i + 1) * chunk], (chunk, W_GROUP)) for i in range(rows // chunk)],
        axis=0)
    gsum = gavg * 64.0
    qn, kn, vv = yq[:, 0:256], yq[:, 256:512], yq[:, 512:768]
    qn = qn * lax.rsqrt(_dot(qn * qn, gsum) + EPS) * (A_DK ** -0.5)
    kn = kn * lax.rsqrt(_dot(kn * kn, gsum) + EPS)
    e_cum = jnp.exp(g_cum)
    rhs_v = vv * beta
    k_beta = kn * beta
    rhs_k = k_beta * e_cum
    q_dec_full = qn * e_cum
    k_dec_full = kn * jnp.exp(g_end - g_cum)

    ri = lax.broadcasted_iota(jnp.int32, (STACK, STACK), 0)
    ci = lax.broadcasted_iota(jnp.int32, (STACK, STACK), 1)
    same = (ri // chunk) == (ci // chunk)
    causal = same & (ri >= ci)
    strict = same & (ri > ci)
    eye = (ri == ci).astype(F32)
    erow = lax.broadcasted_iota(jnp.int32, (STACK, groups * 64), 0)
    ecol = lax.broadcasted_iota(jnp.int32, (STACK, groups * 64), 1)
    expand_mask = (erow // chunk) == (ecol // 64)
    trow = lax.broadcasted_iota(jnp.int32, (groups * 64, STACK), 0)
    tcol = lax.broadcasted_iota(jnp.int32, (groups * 64, STACK), 1)
    expand_mask_t = (trow // 64) == (tcol // chunk)
    eye64 = (lax.broadcasted_iota(jnp.int32, (64, 64), 0)
             == lax.broadcasted_iota(jnp.int32, (64, 64), 1)).astype(BF16)

    def stack_heads(x):
        return jnp.concatenate([x[:, h * 64:(h + 1) * 64] for h in range(A_HEADS)], axis=0)

    def expand(x):
        return jnp.where(expand_mask, jnp.concatenate([x] * groups, axis=1), 0.0)

    def transpose64(x):
        return lax.dot_general(eye64, x.astype(BF16), (((1,), (1,)), ((), ())), preferred_element_type=F32)

    cs = range(nchunks)
    sls = [slice(c * 64, (c + 1) * 64) for c in cs]
    q = [stack_heads(qn[sl]) for sl in sls]
    k = [stack_heads(kn[sl]) for sl in sls]
    g_s = [stack_heads(g_cum[sl]) for sl in sls]
    g_c = [jnp.concatenate([x] * A_HEADS, axis=1) for x in g_s]
    g_r = [x.T for x in g_c]
    decay = [jnp.exp(jnp.where(causal, g_c[c] - g_r[c], NEG)) for c in cs]
    amat = [jnp.where(strict, _dot_nt(stack_heads(k_beta[sls[c]]), k[c]) * decay[c], 0.0) for c in cs]
    qk = [_dot_nt(q[c], k[c]) * decay[c] for c in cs]
    x_inv = [eye - jnp.where((ri // 2) == (ci // 2), amat[c], 0.0) for c in cs]
    blk = 2
    while blk < chunk:
        merge = ((ri // (2 * blk)) == (ci // (2 * blk))) & ((ri // blk) != (ci // blk))
        left = [_dot(x_inv[c], jnp.where(merge, amat[c], 0.0)) for c in cs]
        if side_work:
            side_work.pop(0)()
        x_inv = [x_inv[c] - _dot(left[c], x_inv[c]) for c in cs]
        blk *= 2
    while side_work:
        side_work.pop(0)()
    sol = [_dot(x_inv[c], jnp.concatenate([stack_heads(rhs_v[sls[c]]), stack_heads(rhs_k[sls[c]])], axis=1))
           for c in cs]
    w_val = [x[:, :64] for x in sol]
    w_key = [x[:, 64:] for x in sol]
    q_dec = [stack_heads(q_dec_full[sl]) for sl in sls]
    kd_t = [transpose64(stack_heads(k_dec_full[sl])) for sl in sls]
    kd_t_exp = [jnp.where(expand_mask_t, jnp.concatenate([x] * groups, axis=0), 0.0) for x in kd_t]
    g_end_s = [stack_heads(g_end[sl]) for sl in sls]
    s_decay = [jnp.exp(jnp.concatenate(
        [jnp.broadcast_to(x[g * chunk:g * chunk + 1], (64, 64)) for g in range(groups)], axis=0)) for x in g_end_s]

    state = s_scr[...]
    o_chunks = []
    if nchunks == 1:
        s_b = state.astype(BF16)
        u = w_val[0] - _dot(expand(w_key[0]), s_b)
        o_chunks.append(_dot(expand(q_dec[0]), s_b) + _dot(qk[0], u))
        state = state * s_decay[0] + _dot_nt(kd_t_exp[0], transpose64(u))
    else:
        q_eff = [expand(q_dec[c] - _dot(qk[c], w_key[c])) for c in cs]
        o_free = [_dot(qk[c], w_val[c]) for c in cs]
        s_mix = [_dot(kd_t_exp[c], expand(w_key[c])) for c in cs]
        s_add = [_dot(kd_t_exp[c], w_val[c]) for c in cs]
        for c in cs:
            s_b = state.astype(BF16)
            o_chunks.append(_dot(q_eff[c], s_b) + o_free[c])
            state = state * s_decay[c] - _dot(s_mix[c], s_b) + s_add[c]
    s_scr[...] = state
    oa = [jnp.concatenate([o[h * 64:(h + 1) * 64] for h in range(A_HEADS)], axis=1) for o in o_chunks]
    oa = oa[0] if nchunks == 1 else jnp.concatenate(oa, axis=0)
    oa = oa * lax.rsqrt(_dot(oa * oa, gavg) + EPS) * onorm_ref[...] * _silu(z[:, Z_GATE:Z_B])

    mix = jnp.concatenate([oa, results["ob"], results["oc"], results["od"]], axis=1).astype(mix_ref.dtype)
    if nseg == 1:
        mix_ref[0] = mix
    else:
        mix_ref[...] = mix.reshape(nseg, tn, 4 * W_GROUP)

    @pl.when(t == nt - 1)
    def _store_state():
        for s in range(nseg):
            dconv_o[s] = xq_scr[s, qoff - (A_CONV - 1):qoff, :]
            bconv_o[s] = xb_scr[s, boff - (B_CONV - 1):boff, :]
            pbuf_o[s] = xc_scr[s, coff - POOL_BUF:coff, :]
            lconv_o[s] = xd_scr[s, doff - (D_CONV - 1):doff, :]
            lh_o[s] = h_scr[s]
            for h in range(A_HEADS):
                g = h * seq_per_chunk + s
                ds_o[s, h] = s_scr[g * 64:(g + 1) * 64, :]


def _mixer(x, state, lw, *, nseg, tn, chunk, pos0):
    bsz, tlen, d = x.shape
    dconv, ds, bconv, pbuf, lconv, lh = state
    rows = nseg * tn
    assert rows % 64 == 0 and (nseg == 1 or rows == 64) and tlen % tn == 0 and bsz % nseg == 0
    assert tn % SUBLANES == 0
    assert (nseg == 1 and chunk == 64) or (nseg > 1 and chunk == tn and tlen == tn)
    groups = STACK // chunk
    grid = (bsz // nseg, tlen // tn)

    def state_spec(a):
        blk = (nseg,) + a.shape[1:]
        nd = a.ndim
        return pl.BlockSpec(blk, lambda b, t: (b,) + (0,) * (nd - 1))

    weights = (lw["norm_mix_pre"], lw["w_in"], lw["conv_qkv"], lw["hp"], lw["onorm"], lw["dw_b"], lw["brow"],
               lw["w_pw"], lw["w_pool"], lw["conv_d"], lw["w_rg"], lw["w_ig"], lw["gavg"])
    states = (dconv, ds, bconv, pbuf, lconv, lh)
    out_shape = (jax.ShapeDtypeStruct((bsz, tlen, 4 * W_GROUP), F32),) + tuple(
        jax.ShapeDtypeStruct(a.shape, a.dtype) for a in states)
    kern = functools.partial(_mixer_kernel, nseg=nseg, tn=tn, chunk=chunk, pos0=pos0)
    return pl.pallas_call(
        kern,
        grid=grid,
        in_specs=[pl.BlockSpec((nseg, tn, d), lambda b, t: (b, t, 0))]
        + [state_spec(a) for a in states] + [_const_spec(w.shape) for w in weights],
        out_specs=[pl.BlockSpec((nseg, tn, 4 * W_GROUP), lambda b, t: (b, t, 0))]
        + [state_spec(a) for a in states],
        out_shape=out_shape,
        scratch_shapes=[
            pltpu.VMEM((nseg, 8 + tn + SUBLANES, A_QKV), F32),
            pltpu.VMEM((nseg, 32 + tn + SUBLANES, W_GROUP), F32),
            pltpu.VMEM((nseg, 16 + tn, W_GROUP), F32),
            pltpu.VMEM((nseg, 8 + tn + SUBLANES, W_GROUP), F32),
            pltpu.VMEM((groups * 64, 64), F32),
            pltpu.VMEM((nseg, 1, W_GROUP), F32),
        ],
        compiler_params=_cparams("parallel", "arbitrary"),
        name="mixer",
    )(x, *states, *weights)


def _post_mixer_kernel(mix_ref, x_ref, wout_ref, gpost_ref, gpre_ref, wq_ref, x1_ref, q_ref):
    x1 = x_ref[...] + _rms(_dot(mix_ref[...], wout_ref[...]), gpost_ref[...])
    x1_ref[...] = x1
    q_ref[...] = _dot(_rms(x1, gpre_ref[...]), wq_ref[...]).astype(q_ref.dtype)


def _post_mixer(mix, x, lw, tm):
    m, d = x.shape
    row = lambda i: (i, 0)
    return pl.pallas_call(
        _post_mixer_kernel,
        grid=(m // tm,),
        in_specs=[pl.BlockSpec((tm, d), row), pl.BlockSpec((tm, d), row), _const_spec((d, d)),
                  _const_spec((1, d)), _const_spec((1, d)), _const_spec((d, d))],
        out_specs=[pl.BlockSpec((tm, d), row), pl.BlockSpec((tm, d), row)],
        out_shape=(jax.ShapeDtypeStruct((m, d), F32), jax.ShapeDtypeStruct((m, d), BF16)),
        compiler_params=_cparams("parallel"),
        name="post_mixer",
    )(mix, x, lw["w_out"], lw["norm_mix_post"], lw["norm_x_pre"], lw["w_xq"])


def _softmax_rows(s):
    e = jnp.exp(s - jnp.max(s, axis=-1, keepdims=True))
    return e / jnp.sum(e, axis=-1, keepdims=True)


def _attn_kernel(q_ref, k_ref, v_ref, o_ref):
    q = q_ref[0]
    outs = []
    for h in range(X_HEADS):
        sl = slice(h * X_HD, (h + 1) * X_HD)
        p = _softmax_rows(_dot_nt(q[:, sl], k_ref[0, :, sl]) * (X_HD ** -0.5))
        outs.append(_dot(p, v_ref[0, :, sl]))
    o_ref[0] = jnp.concatenate(outs, axis=1).astype(o_ref.dtype)


def _attn(q, k, v, tq):
    bsz, tlen, d = q.shape
    tile = lambda b, t: (b, t, 0)
    mem = lambda b, t: (b, 0, 0)
    return pl.pallas_call(
        _attn_kernel,
        grid=(bsz, tlen // tq),
        in_specs=[pl.BlockSpec((1, tq, d), tile), pl.BlockSpec((1, N_MEM, d), mem),
                  pl.BlockSpec((1, N_MEM, d), mem)],
        out_specs=pl.BlockSpec((1, tq, d), tile),
        out_shape=jax.ShapeDtypeStruct((bsz, tlen, d), BF16),
        compiler_params=_cparams("parallel", "arbitrary"),
        name="attn",
    )(q, k, v)


def _attn_cache_kernel(q_ref, k_ref, v_ref, o_ref, *, nseq, tlen):
    hrows = X_HEADS * tlen
    qrow_head = lax.broadcasted_iota(jnp.int32, (hrows, N_MEM * X_HEADS), 0) // tlen
    col_head = lax.broadcasted_iota(jnp.int32, (hrows, N_MEM * X_HEADS), 1) % X_HEADS
    own = qrow_head == col_head
    for s in range(nseq):
        k2 = k_ref[0, s].reshape(N_MEM * X_HEADS, X_HD)
        v2 = v_ref[0, s].reshape(N_MEM * X_HEADS, X_HD)
        qs = q_ref[s].astype(F32)
        q4 = jnp.concatenate([qs[:, h * X_HD:(h + 1) * X_HD] for h in range(X_HEADS)], axis=0)
        p = _softmax_rows(jnp.where(own, _dot_nt(q4, k2) * (X_HD ** -0.5), NEG))
        o4 = _dot(p, v2)
        o_ref[s] = jnp.concatenate([o4[h * tlen:(h + 1) * tlen] for h in range(X_HEADS)], axis=1)


def _attn_cache(q, cache_k, cache_v, layer, nseq):
    bsz, tlen, d = q.shape
    cache_blk = (1, nseq) + cache_k.shape[2:]
    cache_map = lambda i: (layer, i, 0, 0, 0)
    kern = functools.partial(_attn_cache_kernel, nseq=nseq, tlen=tlen)
    return pl.pallas_call(
        kern,
        grid=(bsz // nseq,),
        in_specs=[pl.BlockSpec((nseq, tlen, d), lambda i: (i, 0, 0)), pl.BlockSpec(cache_blk, cache_map),
                  pl.BlockSpec(cache_blk, cache_map)],
        out_specs=pl.BlockSpec((nseq, tlen, d), lambda i: (i, 0, 0)),
        out_shape=jax.ShapeDtypeStruct((bsz, tlen, d), F32),
        compiler_params=_cparams("parallel"),
        name="attn_cache",
    )(q, cache_k, cache_v)


def _xo_ffn_kernel(o_ref, x_ref, wo_ref, gxo_ref, gpre_ref, win_ref, wout_ref, gpost_ref, y_ref):
    x = x_ref[...] + _rms(_dot(o_ref[...], wo_ref[...]), gxo_ref[...])
    gu = _dot(_rms(x, gpre_ref[...]), win_ref[...])
    act = _silu(gu[:, :FFN]) * gu[:, FFN:]
    y_ref[...] = x + _rms(_dot(act, wout_ref[...]), gpost_ref[...])


def _xo_ffn(o, x, lw, tm):
    m, d = x.shape
    row = lambda i: (i, 0)
    return pl.pallas_call(
        _xo_ffn_kernel,
        grid=(m // tm,),
        in_specs=[pl.BlockSpec((tm, d), row), pl.BlockSpec((tm, d), row), _const_spec((d, d)), _const_spec((1, d)),
                  _const_spec((1, d)), _const_spec((d, 2 * FFN)), _const_spec((FFN, d)), _const_spec((1, d))],
        out_specs=pl.BlockSpec((tm, d), row),
        out_shape=jax.ShapeDtypeStruct((m, d), F32),
        compiler_params=_cparams("parallel"),
        name="xo_ffn",
    )(o, x, lw["w_xo"], lw["norm_x_post"], lw["norm_ffn_pre"], lw["w_ffn_in"], lw["w_ffn_out"],
      lw["norm_ffn_post"])


def _block_diag(w):
    n, k, _ = w.shape
    out = jnp.zeros((n * k, n * k), w.dtype)
    for i in range(n):
        out = out.at[i * k:(i + 1) * k, i * k:(i + 1) * k].set(w[i])
    return out


def _prep_layer(p):
    w_in = p["w_in"]
    off_araw, off_gate = A_QKV, A_QKV + 2 * A_HEADS
    per_head = lambda a: jnp.repeat(a, 64, axis=-1)
    w_in = jnp.concatenate(
        [w_in[:, :off_araw], w_in[:, off_gate:], per_head(w_in[:, off_araw:off_araw + A_HEADS]),
         per_head(w_in[:, off_araw + A_HEADS:off_gate])], axis=1)
    row = lambda a: a[None, :]
    lw = {
        "w_in": w_in.astype(BF16),
        "norm_mix_pre": row(p["norm_mix_pre"]),
        "conv_qkv": p["conv_qkv"],
        "hp": jnp.stack([per_head(p["a_log"]), per_head(p["dt_bias"])], axis=0),
        "onorm": row(jnp.tile(p["onorm_a"], A_HEADS)),
        "dw_b": p["dw_b"],
        "brow": jnp.stack([p["dwbias_b"], p["gn_gain_b"], p["gn_bias_b"], p["scale_pool"],
                           p["conv_bias_d"], p["b_rg"], p["b_ig"], p["lam_d"]], axis=0),
        "w_pw": p["w_pw_b"].astype(BF16),
        "w_pool": _block_diag(p["w_pool"]).astype(BF16),
        "conv_d": p["conv_d"],
        "w_rg": _block_diag(p["w_rg"]).astype(BF16),
        "w_ig": _block_diag(p["w_ig"]).astype(BF16),
        "gavg": _block_diag(jnp.full((B_GROUPS, 64, 64), 1.0 / 64, F32)).astype(BF16),
        "w_out": p["w_out"].astype(BF16),
        "norm_mix_post": row(p["norm_mix_post"]),
        "norm_x_pre": row(p["norm_x_pre"]),
        "w_xq": p["w_xq"].astype(BF16),
        "w_xo": p["w_xo"].astype(BF16),
        "norm_x_post": row(p["norm_x_post"]),
        "norm_ffn_pre": row(p["norm_ffn_pre"]),
        "w_ffn_in": p["w_ffn_in"].astype(BF16),
        "w_ffn_out": p["w_ffn_out"].astype(BF16),
        "norm_ffn_post": row(p["norm_ffn_post"]),
    }
    return lw


def _layer(x, attend, state, lw, *, nseg, tn, chunk, pos0, tm):
    bsz, tlen, d = x.shape
    xf = x.reshape(bsz * tlen, d)
    mix, *new_state = _mixer(x, state, lw, nseg=nseg, tn=tn, chunk=chunk, pos0=pos0)
    x1, q = _post_mixer(mix.reshape(bsz * tlen, d), xf, lw, tm)
    o = attend(q.reshape(bsz, tlen, d))
    x3 = _xo_ffn(o.reshape(bsz * tlen, d), x1, lw, tm)
    return x3.reshape(bsz, tlen, d), new_state


def kernel(x_prompt, x_sample, mem_prompt, state_delta, state_delta_conv, state_conf_conv, state_pool,
           state_lru_conv, state_lru_h, cache_mem_k, cache_mem_v, norm_mix_pre, norm_mix_post, w_in, conv_qkv,
           a_log, dt_bias, onorm_a, dw_b, dwbias_b, gn_gain_b, gn_bias_b, w_pw_b, w_pool, scale_pool, conv_d,
           conv_bias_d, w_rg, b_rg, w_ig, b_ig, lam_d, w_out, norm_x_pre, norm_x_post, norm_mem, w_xq, w_xkv,
           w_xo, norm_ffn_pre, norm_ffn_post, w_ffn_in, w_ffn_out):
    depth = w_in.shape[0]
    bp, tp, d = x_prompt.shape
    bs, ts, _ = x_sample.shape
    dt = x_prompt.dtype
    per_layer = dict(norm_mix_pre=norm_mix_pre, norm_mix_post=norm_mix_post, w_in=w_in, conv_qkv=conv_qkv,
                     a_log=a_log, dt_bias=dt_bias, onorm_a=onorm_a, dw_b=dw_b, dwbias_b=dwbias_b,
                     gn_gain_b=gn_gain_b, gn_bias_b=gn_bias_b, w_pw_b=w_pw_b, w_pool=w_pool,
                     scale_pool=scale_pool, conv_d=conv_d, conv_bias_d=conv_bias_d, w_rg=w_rg, b_rg=b_rg,
                     w_ig=w_ig, b_ig=b_ig, lam_d=lam_d, w_out=w_out, norm_x_pre=norm_x_pre,
                     norm_x_post=norm_x_post, w_xq=w_xq, w_xo=w_xo, norm_ffn_pre=norm_ffn_pre,
                     norm_ffn_post=norm_ffn_post, w_ffn_in=w_ffn_in, w_ffn_out=w_ffn_out)
    mem_flat = mem_prompt.reshape(bp * N_MEM, d)
    zero_state = (jnp.zeros((bp, A_CONV - 1, A_QKV), dt), jnp.zeros((bp, A_HEADS, A_DK, A_DK), dt),
                  jnp.zeros((bp, B_CONV - 1, W_GROUP), dt), jnp.zeros((bp, POOL_BUF, W_GROUP), dt),
                  jnp.zeros((bp, D_CONV - 1, W_GROUP), dt), jnp.zeros((bp, 1, W_GROUP), dt))
    xp, xs = x_prompt, x_sample
    new_p = [[] for _ in range(6)]
    new_s = [[] for _ in range(6)]
    mem_k_list, mem_v_list = [], []
    tn_p = min(tp, 256)
    for l in range(depth):
        lw = _prep_layer({k: v[l] for k, v in per_layer.items()})
        g_mem = norm_mem[l][None, :]
        w_kv = w_xkv[l].astype(BF16)
        mk = _norm_matmul(mem_flat, g_mem, w_kv[:, :d], F32, 256).reshape(bp, N_MEM, d)
        mv = _norm_matmul(mem_flat, g_mem, w_kv[:, d:], F32, 256).reshape(bp, N_MEM, d)
        xp, st_p = _layer(xp, lambda q: _attn(q, mk, mv, 256), zero_state, lw,
                          nseg=1, tn=tn_p, chunk=64, pos0=0, tm=256)
        st_in = (state_delta_conv[l], state_delta[l], state_conf_conv[l], state_pool[l], state_lru_conv[l],
                 state_lru_h[l][:, None, :])
        xs, st_s = _layer(xs, lambda q: _attn_cache(q, cache_mem_k, cache_mem_v, l, 2), st_in, lw,
                          nseg=64 // ts, tn=ts, chunk=ts, pos0=PAST_LEN, tm=256)
        for j in range(6):
            new_p[j].append(st_p[j])
            new_s[j].append(st_s[j])
        mem_k_list.append(mk.reshape(bp, N_MEM, X_HEADS, X_HD))
        mem_v_list.append(mv.reshape(bp, N_MEM, X_HEADS, X_HD))
    dconv_p, delta_p, conf_p, pool_p, lconv_p, lh_p = (jnp.stack(a) for a in new_p)
    dconv_s, delta_s, conf_s, pool_s, lconv_s, lh_s = (jnp.stack(a) for a in new_s)
    return (xp, xs, delta_p, delta_s, dconv_p, dconv_s, conf_p, conf_s, pool_p, pool_s,
            lconv_p, lconv_s, lh_p[:, :, 0, :], lh_s[:, :, 0, :], jnp.stack(mem_k_list), jnp.stack(mem_v_list))
```

```python
import functools

import jax
import jax.numpy as jnp
from jax import lax
from jax.experimental import pallas as pl
from jax.experimental.pallas import tpu as pltpu

F32 = jnp.float32
BF16 = jnp.bfloat16
EPS = 1e-6

D_MODEL = 1024
W_GROUP = 256
A_HEADS = 4
A_DK = 64
A_CONV = 4
A_QKV = 768
B_CONV = 31
B_GROUPS = 4
POOL_BUF = 15
D_CONV = 4
LRU_C = 8.0
N_MEM = 256
X_HEADS = 4
X_HD = 256
FFN = 2816
PAST_LEN = 16384

Z_QKV = 0
Z_GATE = 768
Z_B = 1024
Z_C = 1536
Z_D = 1792
Z_G = 2304
Z_BETA = 2560
Z_COLS = 2816

SUBLANES = 8
STACK = A_HEADS * 64
NEG = -1e30
VMEM_LIMIT = 56 * 1024 * 1024


def _cparams(*sem):
    return pltpu.CompilerParams(dimension_semantics=sem, vmem_limit_bytes=VMEM_LIMIT)


def _const_spec(shape):
    nd = len(shape)
    return pl.BlockSpec(shape, lambda *_: (0,) * nd, pipeline_mode=pl.Buffered(1))


def _rms(x, g):
    return x * lax.rsqrt(jnp.mean(x * x, axis=-1, keepdims=True) + EPS) * g


def _sigmoid(x):
    return 0.5 * jnp.tanh(0.5 * x) + 0.5


def _silu(x):
    return x * _sigmoid(x)


def _softplus(x):
    return jnp.maximum(x, 0.0) + jnp.log(1.0 + jnp.exp(-jnp.abs(x)))


def _dot(a, b):
    return jnp.dot(a.astype(BF16), b.astype(BF16), preferred_element_type=F32)


def _dot_nt(a, b):
    return lax.dot_general(a.astype(BF16), b.astype(BF16), (((1,), (1,)), ((), ())),
                           preferred_element_type=F32)


def _dot_2pass(x, w):
    hi = x.astype(BF16)
    lo = (x - hi.astype(F32)).astype(BF16)
    return (jnp.dot(hi, w, preferred_element_type=F32) + jnp.dot(lo, w, preferred_element_type=F32))


def _memory_kv_kernel(mem_ref, g_ref, w_ref, k_ref, v_ref, k_rows_ref, v_rows_ref):
    kv = _dot(_rms(mem_ref[0], g_ref[...]), w_ref[...])
    k, v = kv[:, :D_MODEL], kv[:, D_MODEL:]
    k_ref[...] = k.reshape(N_MEM, X_HEADS, X_HD)
    v_ref[...] = v.reshape(N_MEM, X_HEADS, X_HD)
    k_rows_ref[...] = k.astype(k_rows_ref.dtype)
    v_rows_ref[...] = v.astype(v_rows_ref.dtype)


def _memory_kv(mem, g, w):
    bsz, n_mem, d = mem.shape
    depth = w.shape[0]
    cache_spec = pl.BlockSpec((None, None, n_mem, X_HEADS, X_HD), lambda l, b: (l, b, 0, 0, 0))
    rows_spec = pl.BlockSpec((None, None, n_mem, d), lambda l, b: (l, b, 0, 0))
    cache_shape = jax.ShapeDtypeStruct((depth, bsz, n_mem, X_HEADS, X_HD), F32)
    rows_shape = jax.ShapeDtypeStruct((depth, bsz, n_mem, d), BF16)
    return pl.pallas_call(
        _memory_kv_kernel,
        grid=(depth, bsz),
        in_specs=[pl.BlockSpec((1, n_mem, d), lambda l, b: (b, 0, 0)),
                  pl.BlockSpec((None, 1, d), lambda l, b: (l, 0, 0)),
                  pl.BlockSpec((None, d, 2 * d), lambda l, b: (l, 0, 0))],
        out_specs=[cache_spec, cache_spec, rows_spec, rows_spec],
        out_shape=[cache_shape, cache_shape, rows_shape, rows_shape],
        compiler_params=_cparams("parallel", "arbitrary"),
        name="memory_kv",
    )(mem, g, w)


def _seg_scan_sum(x, seg_len, row_in_seg):
    s = 1
    while s < seg_len:
        x = x + jnp.where(row_in_seg >= s, pltpu.roll(x, s, axis=0), 0.0)
        s *= 2
    return x


def _linear_scan(a, b, carries, seg_rows):
    rows, width = a.shape
    nblk = rows // SUBLANES
    a = a.reshape(nblk, SUBLANES, width)
    b = b.reshape(nblk, SUBLANES, width)
    sub = lax.broadcasted_iota(jnp.int32, (1, SUBLANES, 1), 1)
    s = 1
    while s < SUBLANES:
        keep = sub >= s
        a_sh = jnp.where(keep, pltpu.roll(a, s, axis=1), 1.0)
        b_sh = jnp.where(keep, pltpu.roll(b, s, axis=1), 0.0)
        b = a * b_sh + b
        a = a * a_sh
        s *= 2
    blk_per_seg = seg_rows // SUBLANES
    outs = []
    carry = None
    for g in range(nblk):
        if g % blk_per_seg == 0:
            carry = carries[g // blk_per_seg]
        hg = a[g] * carry + b[g]
        carry = hg[SUBLANES - 1:SUBLANES]
        outs.append(hg)
    return jnp.concatenate(outs, axis=0)


def _conv_segment(scr, seg, x, w_rows, width, off, tn, row_block):
    scr[seg, off:off + tn, :] = x
    base = off - (width - 1)
    outs = []
    for r0 in range(0, tn, row_block):
        rb = min(row_block, tn - r0)
        acc = None
        for res in range(SUBLANES):
            part = None
            for p in range(base, base + width):
                if p % SUBLANES != res:
                    continue
                lo = p - res + r0
                term = scr[seg, lo:lo + rb + SUBLANES, :] * w_rows[p - base]
                part = term if part is None else part + term
            if part is None:
                continue
            piece = part[res:res + rb]
            acc = piece if acc is None else acc + piece
        outs.append(acc)
    tail = scr[seg, off + tn - (width - 1):off + tn, :]
    scr[seg, off - (width - 1):off, :] = tail
    return outs[0] if len(outs) == 1 else jnp.concatenate(outs, axis=0)


def _pool_segment(scr, seg, u, off, tn):
    scr[seg, off:off + tn, :] = u
    x = scr[seg, 0:off + tn, :]
    s2 = x + pltpu.roll(x, 1, axis=0)
    s4 = s2 + pltpu.roll(s2, 2, axis=0)
    s8 = s4 + pltpu.roll(s4, 4, axis=0)
    s16 = s8[off:] + s8[off - 8:off - 8 + tn]
    lane = lax.broadcasted_iota(jnp.int32, (1, W_GROUP), 1)
    wsum = jnp.where(lane < 64, s2[off:], jnp.where(lane < 128, s4[off:], jnp.where(lane < 192, s8[off:], s16)))
    tail = scr[seg, off + tn - POOL_BUF:off + tn, :]
    scr[seg, off - POOL_BUF:off, :] = tail
    return wsum


def _mixer_kernel(x_ref, dconv_ref, ds_ref, bconv_ref, pbuf_ref, lconv_ref, lh_ref,
                  gin_ref, win_ref, wqkv_ref, hp_ref, onorm_ref, dw_ref, brow_ref, wpw_ref, wpool_ref, convd_ref,
                  wrg_ref, wig_ref, gavg_ref, wout_ref, gpost_ref, gpre_ref, wq_ref,
                  dconv_all, ds_all, bconv_all, pbuf_all, lconv_all, lh_all,
                  x1_ref, q_ref, dconv_o, ds_o, bconv_o, pbuf_o, lconv_o, lh_o,
                  xq_scr, xb_scr, xc_scr, xd_scr, s_scr, h_scr,
                  *, nseg, tn, chunk, pos0):
    del dconv_all, ds_all, bconv_all, pbuf_all, lconv_all, lh_all
    t = pl.program_id(1)
    nt = pl.num_programs(1)
    rows = nseg * tn
    nchunks = rows // 64
    groups = STACK // chunk
    seq_per_chunk = 64 // chunk
    qoff, boff, coff, doff = 8, 32, 16, 8

    @pl.when(t == 0)
    def _load_state():
        for s in range(nseg):
            xq_scr[s, 0:qoff, :] = jnp.zeros((qoff, A_QKV), F32)
            xq_scr[s, qoff + tn:, :] = jnp.zeros((SUBLANES, A_QKV), F32)
            xb_scr[s, 0:SUBLANES, :] = jnp.zeros((SUBLANES, W_GROUP), F32)
            xb_scr[s, boff + tn:, :] = jnp.zeros((SUBLANES, W_GROUP), F32)
            xc_scr[s, 0:SUBLANES, :] = jnp.zeros((SUBLANES, W_GROUP), F32)
            xd_scr[s, 0:doff, :] = jnp.zeros((doff, W_GROUP), F32)
            xd_scr[s, doff + tn:, :] = jnp.zeros((SUBLANES, W_GROUP), F32)
            xq_scr[s, qoff - (A_CONV - 1):qoff, :] = dconv_ref[s]
            xb_scr[s, boff - (B_CONV - 1):boff, :] = bconv_ref[s]
            xc_scr[s, coff - POOL_BUF:coff, :] = pbuf_ref[s]
            xd_scr[s, doff - (D_CONV - 1):doff, :] = lconv_ref[s]
            h_scr[s] = lh_ref[s]
            for h in range(A_HEADS):
                g = h * seq_per_chunk + s
                s_scr[g * 64:(g + 1) * 64, :] = ds_ref[s, h]

    x_in = x_ref[0] if nseg == 1 else x_ref[...].reshape(rows, D_MODEL)
    z = _dot(_rms(x_in, gin_ref[...]), win_ref[...])

    def per_segment(fn, x):
        outs = [fn(s, x[s * tn:(s + 1) * tn]) for s in range(nseg)]
        return outs[0] if nseg == 1 else jnp.concatenate(outs, axis=0)

    row = lax.broadcasted_iota(jnp.int32, (rows, 1), 0)
    row_in_seg = row % tn
    brow = brow_ref[...]
    gavg = gavg_ref[...]
    results = {}

    def section_b_conv():
        ub = z[:, Z_B:Z_C]
        glu = ub[:, :W_GROUP] * _sigmoid(ub[:, W_GROUP:])
        dw_rows = [dw_ref[j:j + 1, :] for j in range(B_CONV)]
        yb = per_segment(lambda s, x: _conv_segment(xb_scr, s, x, dw_rows, B_CONV, boff, tn, 64), glu)
        results["yb"] = yb + brow[0:1]

    def section_b_norm():
        yb = results["yb"]
        yc = yb - _dot_2pass(yb, gavg)
        var = _dot(yc * yc, gavg)
        yn = yc * lax.rsqrt(var + EPS) * brow[1:2] + brow[2:3]
        results["ob"] = _dot(_silu(yn), wpw_ref[...])

    def section_c():
        uc = z[:, Z_C:Z_D]
        wsum = per_segment(lambda s, x: _pool_segment(xc_scr, s, x, coff, tn), uc)
        lane = lax.broadcasted_iota(jnp.int32, (1, W_GROUP), 1)
        wlen = jnp.where(lane < 64, 2.0, jnp.where(lane < 128, 4.0, jnp.where(lane < 192, 8.0, 16.0)))
        pos = (pos0 + t * tn + row_in_seg).astype(F32)
        cnt = jnp.minimum(pos + 1.0, wlen)
        results["oc"] = _dot(wsum / cnt - uc, wpool_ref[...]) * brow[3:4]

    def section_d_gates():
        ud = z[:, Z_D:Z_G]
        results["gate_d"] = jax.nn.gelu(ud[:, :W_GROUP])
        cd_rows = [convd_ref[j:j + 1, :] for j in range(D_CONV)]
        xr = per_segment(lambda s, x: _conv_segment(xd_scr, s, x, cd_rows, D_CONV, doff, tn, 64), ud[:, W_GROUP:])
        xr = xr + brow[4:5]
        r_gate = _sigmoid(_dot(xr, wrg_ref[...]) + brow[5:6])
        i_gate = _sigmoid(_dot(xr, wig_ref[...]) + brow[6:7])
        log_a = -LRU_C * r_gate * _softplus(-brow[7:8])
        a = jnp.exp(log_a)
        results["lru_a"] = a
        results["lru_b"] = jnp.sqrt(-jnp.tanh(log_a) * (a * a + 1.0)) * (i_gate * xr)

    def section_d_scan():
        hseq = _linear_scan(results["lru_a"], results["lru_b"], [h_scr[s] for s in range(nseg)], tn)
        results["od"] = results["gate_d"] * hseq
        for s in range(nseg):
            h_scr[s] = hseq[(s + 1) * tn - 1:(s + 1) * tn, :]

    side_work = [section_b_conv, section_b_norm, section_c, section_d_gates, section_d_scan]

    wq_rows = [wqkv_ref[j:j + 1, :] for j in range(A_CONV)]
    yq = per_segment(lambda s, x: _conv_segment(xq_scr, s, x, wq_rows, A_CONV, qoff, tn, 64), z[:, Z_QKV:Z_GATE])
    yq = _silu(yq)
    hp = hp_ref[...]
    g_log = -jnp.exp(hp[0:1]) * _softplus(z[:, Z_G:Z_BETA] + hp[1:2])
    beta = _sigmoid(z[:, Z_BETA:Z_COLS])
    g_cum = _seg_scan_sum(g_log, chunk, row % chunk)
    g_end = jnp.concatenate(
        [jnp.broadcast_to(g_cum[(i + 1) * chunk - 1:(i + 1) * chunk], (chunk, W_GROUP)) for i in range(rows // chunk)],
        axis=0)
    gsum = gavg * 64.0
    qn, kn, vv = yq[:, 0:256], yq[:, 256:512], yq[:, 512:768]
    qn = qn * lax.rsqrt(_dot(qn * qn, gsum) + EPS) * (A_DK ** -0.5)
    kn = kn * lax.rsqrt(_dot(kn * kn, gsum) + EPS)
    e_cum = jnp.exp(g_cum)
    rhs_v = vv * beta
    k_beta = kn * beta
    rhs_k = k_beta * e_cum
    q_dec_full = qn * e_cum
    k_dec_full = kn * jnp.exp(g_end - g_cum)

    ri = lax.broadcasted_iota(jnp.int32, (STACK, STACK), 0)
    ci = lax.broadcasted_iota(jnp.int32, (STACK, STACK), 1)
    same = (ri // chunk) == (ci // chunk)
    causal = same & (ri >= ci)
    strict = same & (ri > ci)
    eye = (ri == ci).astype(F32)
    erow = lax.broadcasted_iota(jnp.int32, (STACK, groups * 64), 0)
    ecol = lax.broadcasted_iota(jnp.int32, (STACK, groups * 64), 1)
    expand_mask = (erow // chunk) == (ecol // 64)
    trow = lax.broadcasted_iota(jnp.int32, (groups * 64, STACK), 0)
    tcol = lax.broadcasted_iota(jnp.int32, (groups * 64, STACK), 1)
    expand_mask_t = (trow // 64) == (tcol // chunk)
    eye64 = (lax.broadcasted_iota(jnp.int32, (64, 64), 0)
             == lax.broadcasted_iota(jnp.int32, (64, 64), 1)).astype(BF16)

    def stack_heads(x):
        return jnp.concatenate([x[:, h * 64:(h + 1) * 64] for h in range(A_HEADS)], axis=0)

    def expand(x):
        return jnp.where(expand_mask, jnp.concatenate([x] * groups, axis=1), 0.0)

    def transpose64(x):
        return lax.dot_general(eye64, x.astype(BF16), (((1,), (1,)), ((), ())), preferred_element_type=F32)

    cs = range(nchunks)
    sls = [slice(c * 64, (c + 1) * 64) for c in cs]
    q = [stack_heads(qn[sl]) for sl in sls]
    k = [stack_heads(kn[sl]) for sl in sls]
    g_s = [stack_heads(g_cum[sl]) for sl in sls]
    g_c = [jnp.concatenate([x] * A_HEADS, axis=1) for x in g_s]
    g_r = [x.T for x in g_c]
    decay = [jnp.exp(jnp.where(causal, g_c[c] - g_r[c], NEG)) for c in cs]
    amat = [jnp.where(strict, _dot_nt(stack_heads(k_beta[sls[c]]), k[c]) * decay[c], 0.0) for c in cs]
    qk = [_dot_nt(q[c], k[c]) * decay[c] for c in cs]
    x_inv = [eye - jnp.where((ri // 2) == (ci // 2), amat[c], 0.0) for c in cs]
    blk = 2
    while blk < chunk:
        merge = ((ri // (2 * blk)) == (ci // (2 * blk))) & ((ri // blk) != (ci // blk))
        left = [_dot(x_inv[c], jnp.where(merge, amat[c], 0.0)) for c in cs]
        if side_work:
            side_work.pop(0)()
        x_inv = [x_inv[c] - _dot(left[c], x_inv[c]) for c in cs]
        blk *= 2
    while side_work:
        side_work.pop(0)()
    sol = [_dot(x_inv[c], jnp.concatenate([stack_heads(rhs_v[sls[c]]), stack_heads(rhs_k[sls[c]])], axis=1))
           for c in cs]
    w_val = [x[:, :64] for x in sol]
    w_key = [x[:, 64:] for x in sol]
    q_dec = [stack_heads(q_dec_full[sl]) for sl in sls]
    kd_t = [transpose64(stack_heads(k_dec_full[sl])) for sl in sls]
    kd_t_exp = [jnp.where(expand_mask_t, jnp.concatenate([x] * groups, axis=0), 0.0) for x in kd_t]
    g_end_s = [stack_heads(g_end[sl]) for sl in sls]
    s_decay = [jnp.exp(jnp.concatenate(
        [jnp.broadcast_to(x[g * chunk:g * chunk + 1], (64, 64)) for g in range(groups)], axis=0)) for x in g_end_s]

    state = s_scr[...]
    o_chunks = []
    if nchunks == 1:
        s_b = state.astype(BF16)
        u = w_val[0] - _dot(expand(w_key[0]), s_b)
        o_chunks.append(_dot(expand(q_dec[0]), s_b) + _dot(qk[0], u))
        state = state * s_decay[0] + _dot_nt(kd_t_exp[0], transpose64(u))
    else:
        q_eff = [expand(q_dec[c] - _dot(qk[c], w_key[c])) for c in cs]
        o_free = [_dot(qk[c], w_val[c]) for c in cs]
        s_mix = [_dot(kd_t_exp[c], expand(w_key[c])) for c in cs]
        s_add = [_dot(kd_t_exp[c], w_val[c]) for c in cs]
        for c in cs:
            s_b = state.astype(BF16)
            o_chunks.append(_dot(q_eff[c], s_b) + o_free[c])
            state = state * s_decay[c] - _dot(s_mix[c], s_b) + s_add[c]
    s_scr[...] = state
    oa = [jnp.concatenate([o[h * 64:(h + 1) * 64] for h in range(A_HEADS)], axis=1) for o in o_chunks]
    oa = oa[0] if nchunks == 1 else jnp.concatenate(oa, axis=0)
    oa = oa * lax.rsqrt(_dot(oa * oa, gavg) + EPS) * onorm_ref[...] * _silu(z[:, Z_GATE:Z_B])

    mix = jnp.concatenate([oa, results["ob"], results["oc"], results["od"]], axis=1)
    x1 = x_in + _rms(_dot(mix, wout_ref[...]), gpost_ref[...])
    x1_ref[...] = x1
    q_ref[...] = _dot(_rms(x1, gpre_ref[...]), wq_ref[...]).astype(q_ref.dtype)

    @pl.when(t == nt - 1)
    def _store_state():
        for s in range(nseg):
            dconv_o[s] = xq_scr[s, qoff - (A_CONV - 1):qoff, :]
            bconv_o[s] = xb_scr[s, boff - (B_CONV - 1):boff, :]
            pbuf_o[s] = xc_scr[s, coff - POOL_BUF:coff, :]
            lconv_o[s] = xd_scr[s, doff - (D_CONV - 1):doff, :]
            lh_o[s] = h_scr[s]
            for h in range(A_HEADS):
                g = h * seq_per_chunk + s
                ds_o[s, h] = s_scr[g * 64:(g + 1) * 64, :]


def _mixer(x, state_in, layer_in, state_out, layer_out, lw, *, nseg, tn, chunk, pos0):
    bsz, tlen, d = x.shape
    rows = nseg * tn
    assert rows % 64 == 0 and (nseg == 1 or rows == 64) and tlen % tn == 0 and bsz % nseg == 0
    assert tn % SUBLANES == 0
    assert (nseg == 1 and chunk == 64) or (nseg > 1 and chunk == tn and tlen == tn)
    groups = STACK // chunk
    grid = (bsz // nseg, tlen // tn)

    def state_spec(a, layer):
        blk = (None, nseg) + a.shape[2:]
        nd = a.ndim
        return pl.BlockSpec(blk, lambda b, t: (layer, b) + (0,) * (nd - 2))

    weights = (lw["norm_mix_pre"], lw["w_in"], lw["conv_qkv"], lw["hp"], lw["onorm"], lw["dw_b"], lw["brow"],
               lw["w_pw"], lw["w_pool"], lw["conv_d"], lw["w_rg"], lw["w_ig"], lw["gavg"],
               lw["w_out"], lw["norm_mix_post"], lw["norm_x_pre"], lw["w_xq"])
    n_in = 1 + len(state_in) + len(weights)
    nt = tlen // tn
    row_spec = pl.BlockSpec((rows, d), lambda b, t: (b * nt + t, 0))
    out_shape = (jax.ShapeDtypeStruct((bsz * tlen, d), F32), jax.ShapeDtypeStruct((bsz * tlen, d), BF16)) + tuple(
        jax.ShapeDtypeStruct(a.shape, a.dtype) for a in state_out)
    kern = functools.partial(_mixer_kernel, nseg=nseg, tn=tn, chunk=chunk, pos0=pos0)
    return pl.pallas_call(
        kern,
        grid=grid,
        in_specs=[pl.BlockSpec((nseg, tn, d), lambda b, t: (b, t, 0))]
        + [state_spec(a, layer_in) for a in state_in] + [_const_spec(w.shape) for w in weights]
        + [pl.BlockSpec(memory_space=pl.ANY) for _ in state_out],
        out_specs=[row_spec, row_spec] + [state_spec(a, layer_out) for a in state_out],
        out_shape=out_shape,
        input_output_aliases={n_in + j: 2 + j for j in range(len(state_out))},
        scratch_shapes=[
            pltpu.VMEM((nseg, 8 + tn + SUBLANES, A_QKV), F32),
            pltpu.VMEM((nseg, 32 + tn + SUBLANES, W_GROUP), F32),
            pltpu.VMEM((nseg, 16 + tn, W_GROUP), F32),
            pltpu.VMEM((nseg, 8 + tn + SUBLANES, W_GROUP), F32),
            pltpu.VMEM((groups * 64, 64), F32),
            pltpu.VMEM((nseg, 1, W_GROUP), F32),
        ],
        compiler_params=_cparams("parallel", "arbitrary"),
        name="mixer",
    )(x, *state_in, *weights, *state_out)


def _softmax_rows(s):
    e = jnp.exp(s - jnp.max(s, axis=-1, keepdims=True))
    return e / jnp.sum(e, axis=-1, keepdims=True)


def _attn_kernel(q_ref, k_ref, v_ref, o_ref):
    q = q_ref[0]
    heads = [slice(h * X_HD, (h + 1) * X_HD) for h in range(X_HEADS)]
    s = [_dot_nt(q[:, sl], k_ref[:, sl]) * (X_HD ** -0.5) for sl in heads]
    p = [_softmax_rows(x) for x in s]
    o = [_dot(p[h], v_ref[:, heads[h]]) for h in range(X_HEADS)]
    o_ref[0] = jnp.concatenate(o, axis=1).astype(o_ref.dtype)


def _attn(q, k_rows, v_rows, layer, tq):
    bsz, tlen, d = q.shape
    tile = lambda b, t: (b, t, 0)
    mem_spec = pl.BlockSpec((None, None, N_MEM, d), lambda b, t: (layer, b, 0, 0))
    return pl.pallas_call(
        _attn_kernel,
        grid=(bsz, tlen // tq),
        in_specs=[pl.BlockSpec((1, tq, d), tile), mem_spec, mem_spec],
        out_specs=pl.BlockSpec((1, tq, d), tile),
        out_shape=jax.ShapeDtypeStruct((bsz, tlen, d), BF16),
        compiler_params=_cparams("parallel", "arbitrary"),
        name="attn",
    )(q, k_rows, v_rows)


def _attn_cache_kernel(q_ref, k_ref, v_ref, o_ref, *, nseq, tlen):
    hrows = X_HEADS * tlen
    qrow_head = lax.broadcasted_iota(jnp.int32, (hrows, N_MEM * X_HEADS), 0) // tlen
    col_head = lax.broadcasted_iota(jnp.int32, (hrows, N_MEM * X_HEADS), 1) % X_HEADS
    own = qrow_head == col_head
    seqs = range(nseq)
    qs = [q_ref[s].astype(F32) for s in seqs]
    q4 = [jnp.concatenate([x[:, h * X_HD:(h + 1) * X_HD] for h in range(X_HEADS)], axis=0) for x in qs]
    sc = [_dot_nt(q4[s], k_ref[0, s].reshape(N_MEM * X_HEADS, X_HD)) * (X_HD ** -0.5) for s in seqs]
    p = [_softmax_rows(jnp.where(own, x, NEG)) for x in sc]
    o4 = [_dot(p[s], v_ref[0, s].reshape(N_MEM * X_HEADS, X_HD)) for s in seqs]
    for s in seqs:
        o_ref[s] = jnp.concatenate([o4[s][h * tlen:(h + 1) * tlen] for h in range(X_HEADS)], axis=1)


def _attn_cache(q, cache_k, cache_v, layer, nseq):
    bsz, tlen, d = q.shape
    cache_blk = (1, nseq) + cache_k.shape[2:]
    cache_map = lambda i: (layer, i, 0, 0, 0)
    kern = functools.partial(_attn_cache_kernel, nseq=nseq, tlen=tlen)
    return pl.pallas_call(
        kern,
        grid=(bsz // nseq,),
        in_specs=[pl.BlockSpec((nseq, tlen, d), lambda i: (i, 0, 0)), pl.BlockSpec(cache_blk, cache_map),
                  pl.BlockSpec(cache_blk, cache_map)],
        out_specs=pl.BlockSpec((nseq, tlen, d), lambda i: (i, 0, 0)),
        out_shape=jax.ShapeDtypeStruct((bsz, tlen, d), F32),
        compiler_params=_cparams("parallel"),
        name="attn_cache",
    )(q, cache_k, cache_v)


def _xo_ffn_kernel(o_ref, x_ref, wo_ref, gxo_ref, gpre_ref, win_ref, wout_ref, gpost_ref, y_ref):
    x = x_ref[...] + _rms(_dot(o_ref[...], wo_ref[...]), gxo_ref[...])
    gu = _dot(_rms(x, gpre_ref[...]), win_ref[...])
    act = _silu(gu[:, :FFN]) * gu[:, FFN:]
    y_ref[...] = x + _rms(_dot(act, wout_ref[...]), gpost_ref[...])


def _xo_ffn(o, x, lw, tm):
    m, d = x.shape
    tm = min(tm, m)
    assert m % tm == 0
    row = lambda i: (i, 0)
    return pl.pallas_call(
        _xo_ffn_kernel,
        grid=(m // tm,),
        in_specs=[pl.BlockSpec((tm, d), row), pl.BlockSpec((tm, d), row), _const_spec((d, d)), _const_spec((1, d)),
                  _const_spec((1, d)), _const_spec((d, 2 * FFN)), _const_spec((FFN, d)), _const_spec((1, d))],
        out_specs=pl.BlockSpec((tm, d), row),
        out_shape=jax.ShapeDtypeStruct((m, d), F32),
        compiler_params=_cparams("parallel"),
        name="xo_ffn",
    )(o, x, lw["w_xo"], lw["norm_x_post"], lw["norm_ffn_pre"], lw["w_ffn_in"], lw["w_ffn_out"],
      lw["norm_ffn_post"])


def _block_diag(w):
    n, k, _ = w.shape
    return (w[:, :, None, :] * jnp.eye(n, dtype=w.dtype)[:, None, :, None]).reshape(n * k, n * k)


def _prep_layer(p):
    w_in = p["w_in"]
    off_araw, off_gate = A_QKV, A_QKV + 2 * A_HEADS
    per_head = lambda a: jnp.repeat(a, 64, axis=-1)
    w_in = jnp.concatenate(
        [w_in[:, :off_araw], w_in[:, off_gate:], per_head(w_in[:, off_araw:off_araw + A_HEADS]),
         per_head(w_in[:, off_araw + A_HEADS:off_gate])], axis=1)
    row = lambda a: a[None, :]
    lw = {
        "w_in": w_in.astype(BF16),
        "norm_mix_pre": row(p["norm_mix_pre"]),
        "conv_qkv": p["conv_qkv"],
        "hp": jnp.stack([per_head(p["a_log"]), per_head(p["dt_bias"])], axis=0),
        "onorm": row(jnp.tile(p["onorm_a"], A_HEADS)),
        "dw_b": p["dw_b"],
        "brow": jnp.stack([p["dwbias_b"], p["gn_gain_b"], p["gn_bias_b"], p["scale_pool"],
                           p["conv_bias_d"], p["b_rg"], p["b_ig"], p["lam_d"]], axis=0),
        "w_pw": p["w_pw_b"].astype(BF16),
        "w_pool": _block_diag(p["w_pool"]).astype(BF16),
        "conv_d": p["conv_d"],
        "w_rg": _block_diag(p["w_rg"]).astype(BF16),
        "w_ig": _block_diag(p["w_ig"]).astype(BF16),
        "gavg": _block_diag(jnp.full((B_GROUPS, 64, 64), 1.0 / 64, F32)).astype(BF16),
        "w_out": p["w_out"].astype(BF16),
        "norm_mix_post": row(p["norm_mix_post"]),
        "norm_x_pre": row(p["norm_x_pre"]),
        "w_xq": p["w_xq"].astype(BF16),
        "w_xo": p["w_xo"].astype(BF16),
        "norm_x_post": row(p["norm_x_post"]),
        "norm_ffn_pre": row(p["norm_ffn_pre"]),
        "w_ffn_in": p["w_ffn_in"].astype(BF16),
        "w_ffn_out": p["w_ffn_out"].astype(BF16),
        "norm_ffn_post": row(p["norm_ffn_post"]),
    }
    return lw


def _layer(x, attend, state_in, layer_in, state_out, layer_out, lw, *, nseg, tn, chunk, pos0, tm):
    bsz, tlen, d = x.shape
    x1, q, *new_state = _mixer(x, state_in, layer_in, state_out, layer_out, lw,
                               nseg=nseg, tn=tn, chunk=chunk, pos0=pos0)
    o = attend(q.reshape(bsz, tlen, d))
    x3 = _xo_ffn(o.reshape(bsz * tlen, d), x1, lw, tm)
    return x3.reshape(bsz, tlen, d), new_state


def kernel(x_prompt, x_sample, mem_prompt, state_delta, state_delta_conv, state_conf_conv, state_pool,
           state_lru_conv, state_lru_h, cache_mem_k, cache_mem_v, norm_mix_pre, norm_mix_post, w_in, conv_qkv,
           a_log, dt_bias, onorm_a, dw_b, dwbias_b, gn_gain_b, gn_bias_b, w_pw_b, w_pool, scale_pool, conv_d,
           conv_bias_d, w_rg, b_rg, w_ig, b_ig, lam_d, w_out, norm_x_pre, norm_x_post, norm_mem, w_xq, w_xkv,
           w_xo, norm_ffn_pre, norm_ffn_post, w_ffn_in, w_ffn_out):
    depth = w_in.shape[0]
    bp, tp, d = x_prompt.shape
    bs, ts, _ = x_sample.shape
    dt = x_prompt.dtype
    per_layer = dict(norm_mix_pre=norm_mix_pre, norm_mix_post=norm_mix_post, w_in=w_in, conv_qkv=conv_qkv,
                     a_log=a_log, dt_bias=dt_bias, onorm_a=onorm_a, dw_b=dw_b, dwbias_b=dwbias_b,
                     gn_gain_b=gn_gain_b, gn_bias_b=gn_bias_b, w_pw_b=w_pw_b, w_pool=w_pool,
                     scale_pool=scale_pool, conv_d=conv_d, conv_bias_d=conv_bias_d, w_rg=w_rg, b_rg=b_rg,
                     w_ig=w_ig, b_ig=b_ig, lam_d=lam_d, w_out=w_out, norm_x_pre=norm_x_pre,
                     norm_x_post=norm_x_post, w_xq=w_xq, w_xo=w_xo, norm_ffn_pre=norm_ffn_pre,
                     norm_ffn_post=norm_ffn_post, w_ffn_in=w_ffn_in, w_ffn_out=w_ffn_out)
    mem_k_p, mem_v_p, k_rows, v_rows = _memory_kv(mem_prompt, norm_mem[:, None, :], w_xkv.astype(BF16))

    def state_shapes(b):
        return ((b, A_CONV - 1, A_QKV), (b, A_HEADS, A_DK, A_DK), (b, B_CONV - 1, W_GROUP), (b, POOL_BUF, W_GROUP),
                (b, D_CONV - 1, W_GROUP), (b, 1, W_GROUP))

    zero_state = tuple(jnp.zeros((1,) + s, dt) for s in state_shapes(bp))
    sample_state = (state_delta_conv, state_delta, state_conf_conv, state_pool, state_lru_conv,
                    state_lru_h[:, :, None, :])
    st_p = tuple(jnp.zeros((depth,) + s, dt) for s in state_shapes(bp))
    st_s = tuple(jnp.zeros((depth,) + s, dt) for s in state_shapes(bs))
    xp, xs = x_prompt, x_sample
    tn_p = min(tp, 256)
    for l in range(depth):
        lw = _prep_layer({k: v[l] for k, v in per_layer.items()})
        xp, st_p = _layer(xp, lambda q: _attn(q, k_rows, v_rows, l, min(tp, 512)), zero_state, 0, st_p, l, lw,
                          nseg=1, tn=tn_p, chunk=64, pos0=0, tm=512)
        xs, st_s = _layer(xs, lambda q: _attn_cache(q, cache_mem_k, cache_mem_v, l, 4), sample_state, l, st_s, l, lw,
                          nseg=64 // ts, tn=ts, chunk=ts, pos0=PAST_LEN, tm=512)
    dconv_p, delta_p, conf_p, pool_p, lconv_p, lh_p = st_p
    dconv_s, delta_s, conf_s, pool_s, lconv_s, lh_s = st_s
    return (xp, xs, delta_p, delta_s, dconv_p, dconv_s, conf_p, conf_s, pool_p, pool_s,
            lconv_p, lconv_s, lh_p[:, :, 0, :], lh_s[:, :, 0, :], mem_k_p, mem_v_p)
```

```python
import functools

import jax
import jax.numpy as jnp
from jax import lax
from jax.experimental import pallas as pl
from jax.experimental.pallas import tpu as pltpu

F32 = jnp.float32
BF16 = jnp.bfloat16
EPS = 1e-6

D_MODEL = 1024
W_GROUP = 256
A_HEADS = 4
A_DK = 64
A_CONV = 4
A_QKV = 768
B_CONV = 31
B_GROUPS = 4
POOL_BUF = 15
D_CONV = 4
LRU_C = 8.0
N_MEM = 256
X_HEADS = 4
X_HD = 256
FFN = 2816
PAST_LEN = 16384

Z_QKV = 0
Z_GATE = 768
Z_B = 1024
Z_C = 1536
Z_D = 1792
Z_G = 2304
Z_BETA = 2560
Z_COLS = 2816

SUBLANES = 8
STACK = A_HEADS * 64
NEG = -1e30
VMEM_LIMIT = 56 * 1024 * 1024


def _cparams(*sem):
    return pltpu.CompilerParams(dimension_semantics=sem, vmem_limit_bytes=VMEM_LIMIT)


def _layer_spec(a, layer):
    nd = a.ndim
    return pl.BlockSpec((None,) + a.shape[1:], lambda *_: (layer,) + (0,) * (nd - 1),
                        pipeline_mode=pl.Buffered(1))


def _rms(x, g):
    return x * lax.rsqrt(jnp.mean(x * x, axis=-1, keepdims=True) + EPS) * g


def _sigmoid(x):
    return 0.5 * jnp.tanh(0.5 * x) + 0.5


def _silu(x):
    return x * _sigmoid(x)


def _softplus(x):
    return jnp.maximum(x, 0.0) + jnp.log(1.0 + jnp.exp(-jnp.abs(x)))


def _dot(a, b):
    return jnp.dot(a.astype(BF16), b.astype(BF16), preferred_element_type=F32)


def _dot_nt(a, b):
    return lax.dot_general(a.astype(BF16), b.astype(BF16), (((1,), (1,)), ((), ())),
                           preferred_element_type=F32)


def _dot_2pass(x, w):
    hi = x.astype(BF16)
    lo = (x - hi.astype(F32)).astype(BF16)
    return (jnp.dot(hi, w, preferred_element_type=F32) + jnp.dot(lo, w, preferred_element_type=F32))


def _memory_kv_kernel(mem_ref, g_ref, w_ref, k_ref, v_ref, k_rows_ref, v_rows_ref):
    kv = _dot(_rms(mem_ref[0], g_ref[...]), w_ref[...])
    k, v = kv[:, :D_MODEL], kv[:, D_MODEL:]
    k_ref[...] = k.reshape(N_MEM, X_HEADS, X_HD)
    v_ref[...] = v.reshape(N_MEM, X_HEADS, X_HD)
    k_rows_ref[...] = k.astype(k_rows_ref.dtype)
    v_rows_ref[...] = v.astype(v_rows_ref.dtype)


def _memory_kv(mem, g, w):
    bsz, n_mem, d = mem.shape
    depth = w.shape[0]
    cache_spec = pl.BlockSpec((None, None, n_mem, X_HEADS, X_HD), lambda l, b: (l, b, 0, 0, 0))
    rows_spec = pl.BlockSpec((None, None, n_mem, d), lambda l, b: (l, b, 0, 0))
    cache_shape = jax.ShapeDtypeStruct((depth, bsz, n_mem, X_HEADS, X_HD), F32)
    rows_shape = jax.ShapeDtypeStruct((depth, bsz, n_mem, d), BF16)
    return pl.pallas_call(
        _memory_kv_kernel,
        grid=(depth, bsz),
        in_specs=[pl.BlockSpec((1, n_mem, d), lambda l, b: (b, 0, 0)),
                  pl.BlockSpec((None, 1, d), lambda l, b: (l, 0, 0)),
                  pl.BlockSpec((None, d, 2 * d), lambda l, b: (l, 0, 0))],
        out_specs=[cache_spec, cache_spec, rows_spec, rows_spec],
        out_shape=[cache_shape, cache_shape, rows_shape, rows_shape],
        compiler_params=_cparams("parallel", "arbitrary"),
        name="memory_kv",
    )(mem, g, w)


def _seg_scan_sum(x, seg_len, row_in_seg):
    s = 1
    while s < seg_len:
        x = x + jnp.where(row_in_seg >= s, pltpu.roll(x, s, axis=0), 0.0)
        s *= 2
    return x


def _linear_scan(a, b, carries, seg_rows):
    rows, width = a.shape
    nblk = rows // SUBLANES
    a = a.reshape(nblk, SUBLANES, width)
    b = b.reshape(nblk, SUBLANES, width)
    sub = lax.broadcasted_iota(jnp.int32, (1, SUBLANES, 1), 1)
    s = 1
    while s < SUBLANES:
        keep = sub >= s
        a_sh = jnp.where(keep, pltpu.roll(a, s, axis=1), 1.0)
        b_sh = jnp.where(keep, pltpu.roll(b, s, axis=1), 0.0)
        b = a * b_sh + b
        a = a * a_sh
        s *= 2
    blk_per_seg = seg_rows // SUBLANES
    outs = []
    carry = None
    for g in range(nblk):
        if g % blk_per_seg == 0:
            carry = carries[g // blk_per_seg]
        hg = a[g] * carry + b[g]
        carry = hg[SUBLANES - 1:SUBLANES]
        outs.append(hg)
    return jnp.concatenate(outs, axis=0)


def _conv_segment(scr, seg, x, w_rows, width, off, tn, row_block):
    scr[seg, off:off + tn, :] = x
    base = off - (width - 1)
    outs = []
    for r0 in range(0, tn, row_block):
        rb = min(row_block, tn - r0)
        acc = None
        for res in range(SUBLANES):
            part = None
            for p in range(base, base + width):
                if p % SUBLANES != res:
                    continue
                lo = p - res + r0
                term = scr[seg, lo:lo + rb + SUBLANES, :] * w_rows[p - base]
                part = term if part is None else part + term
            if part is None:
                continue
            piece = part[res:res + rb]
            acc = piece if acc is None else acc + piece
        outs.append(acc)
    tail = scr[seg, off + tn - (width - 1):off + tn, :]
    scr[seg, off - (width - 1):off, :] = tail
    return outs[0] if len(outs) == 1 else jnp.concatenate(outs, axis=0)


def _pool_segment(scr, seg, u, off, tn):
    scr[seg, off:off + tn, :] = u
    x = scr[seg, 0:off + tn, :]
    s2 = x + pltpu.roll(x, 1, axis=0)
    s4 = s2 + pltpu.roll(s2, 2, axis=0)
    s8 = s4 + pltpu.roll(s4, 4, axis=0)
    s16 = s8[off:] + s8[off - 8:off - 8 + tn]
    lane = lax.broadcasted_iota(jnp.int32, (1, W_GROUP), 1)
    wsum = jnp.where(lane < 64, s2[off:], jnp.where(lane < 128, s4[off:], jnp.where(lane < 192, s8[off:], s16)))
    tail = scr[seg, off + tn - POOL_BUF:off + tn, :]
    scr[seg, off - POOL_BUF:off, :] = tail
    return wsum


def _mixer_kernel(x_ref, dconv_ref, ds_ref, bconv_ref, pbuf_ref, lconv_ref, lh_ref,
                  gin_ref, win_ref, wqkv_ref, hp_ref, onorm_ref, dw_ref, brow_ref, wpw_ref, wpool_ref, convd_ref,
                  wrg_ref, wig_ref, gavg_ref, wout_ref, gpost_ref, gpre_ref, wq_ref,
                  dconv_all, ds_all, bconv_all, pbuf_all, lconv_all, lh_all,
                  x1_ref, q_ref, dconv_o, ds_o, bconv_o, pbuf_o, lconv_o, lh_o,
                  xq_scr, xb_scr, xc_scr, xd_scr, s_scr, h_scr,
                  *, nseg, tn, chunk, pos0):
    del dconv_all, ds_all, bconv_all, pbuf_all, lconv_all, lh_all
    t = pl.program_id(1)
    nt = pl.num_programs(1)
    rows = nseg * tn
    nchunks = rows // 64
    groups = STACK // chunk
    seq_per_chunk = 64 // chunk
    qoff, boff, coff, doff = 8, 32, 16, 8

    @pl.when(t == 0)
    def _load_state():
        for s in range(nseg):
            xq_scr[s, 0:qoff, :] = jnp.zeros((qoff, A_QKV), F32)
            xq_scr[s, qoff + tn:, :] = jnp.zeros((SUBLANES, A_QKV), F32)
            xb_scr[s, 0:SUBLANES, :] = jnp.zeros((SUBLANES, W_GROUP), F32)
            xb_scr[s, boff + tn:, :] = jnp.zeros((SUBLANES, W_GROUP), F32)
            xc_scr[s, 0:SUBLANES, :] = jnp.zeros((SUBLANES, W_GROUP), F32)
            xd_scr[s, 0:doff, :] = jnp.zeros((doff, W_GROUP), F32)
            xd_scr[s, doff + tn:, :] = jnp.zeros((SUBLANES, W_GROUP), F32)
            xq_scr[s, qoff - (A_CONV - 1):qoff, :] = dconv_ref[s]
            xb_scr[s, boff - (B_CONV - 1):boff, :] = bconv_ref[s]
            xc_scr[s, coff - POOL_BUF:coff, :] = pbuf_ref[s]
            xd_scr[s, doff - (D_CONV - 1):doff, :] = lconv_ref[s]
            h_scr[s] = lh_ref[s]
            for h in range(A_HEADS):
                g = h * seq_per_chunk + s
                s_scr[g * 64:(g + 1) * 64, :] = ds_ref[s, h]

    x_in = x_ref[0] if nseg == 1 else x_ref[...].reshape(rows, D_MODEL)
    z = _dot(_rms(x_in, gin_ref[...]), win_ref[...])

    def per_segment(fn, x):
        outs = [fn(s, x[s * tn:(s + 1) * tn]) for s in range(nseg)]
        return outs[0] if nseg == 1 else jnp.concatenate(outs, axis=0)

    row = lax.broadcasted_iota(jnp.int32, (rows, 1), 0)
    row_in_seg = row % tn
    brow = brow_ref[...]
    gavg = gavg_ref[...]
    results = {}

    def section_b_conv():
        ub = z[:, Z_B:Z_C]
        glu = ub[:, :W_GROUP] * _sigmoid(ub[:, W_GROUP:])
        dw_rows = [dw_ref[j:j + 1, :] for j in range(B_CONV)]
        yb = per_segment(lambda s, x: _conv_segment(xb_scr, s, x, dw_rows, B_CONV, boff, tn, 64), glu)
        results["yb"] = yb + brow[0:1]

    def section_b_norm():
        yb = results["yb"]
        yc = yb - _dot_2pass(yb, gavg)
        var = _dot(yc * yc, gavg)
        yn = yc * lax.rsqrt(var + EPS) * brow[1:2] + brow[2:3]
        results["ob"] = _dot(_silu(yn), wpw_ref[...])

    def section_c():
        uc = z[:, Z_C:Z_D]
        wsum = per_segment(lambda s, x: _pool_segment(xc_scr, s, x, coff, tn), uc)
        lane = lax.broadcasted_iota(jnp.int32, (1, W_GROUP), 1)
        wlen = jnp.where(lane < 64, 2.0, jnp.where(lane < 128, 4.0, jnp.where(lane < 192, 8.0, 16.0)))
        pos = (pos0 + t * tn + row_in_seg).astype(F32)
        cnt = jnp.minimum(pos + 1.0, wlen)
        results["oc"] = _dot(wsum / cnt - uc, wpool_ref[...]) * brow[3:4]

    def section_d_gates():
        ud = z[:, Z_D:Z_G]
        results["gate_d"] = jax.nn.gelu(ud[:, :W_GROUP])
        cd_rows = [convd_ref[j:j + 1, :] for j in range(D_CONV)]
        xr = per_segment(lambda s, x: _conv_segment(xd_scr, s, x, cd_rows, D_CONV, doff, tn, 64), ud[:, W_GROUP:])
        xr = xr + brow[4:5]
        r_gate = _sigmoid(_dot(xr, wrg_ref[...]) + brow[5:6])
        i_gate = _sigmoid(_dot(xr, wig_ref[...]) + brow[6:7])
        log_a = -LRU_C * r_gate * _softplus(-brow[7:8])
        a = jnp.exp(log_a)
        results["lru_a"] = a
        results["lru_b"] = jnp.sqrt(-jnp.tanh(log_a) * (a * a + 1.0)) * (i_gate * xr)

    def section_d_scan():
        hseq = _linear_scan(results["lru_a"], results["lru_b"], [h_scr[s] for s in range(nseg)], tn)
        results["od"] = results["gate_d"] * hseq
        for s in range(nseg):
            h_scr[s] = hseq[(s + 1) * tn - 1:(s + 1) * tn, :]

    side_work = [section_b_conv, section_b_norm, section_c, section_d_gates, section_d_scan]

    wq_rows = [wqkv_ref[j:j + 1, :] for j in range(A_CONV)]
    yq = per_segment(lambda s, x: _conv_segment(xq_scr, s, x, wq_rows, A_CONV, qoff, tn, 64), z[:, Z_QKV:Z_GATE])
    yq = _silu(yq)
    hp = hp_ref[...]
    g_log = -jnp.exp(hp[0:1]) * _softplus(z[:, Z_G:Z_BETA] + hp[1:2])
    beta = _sigmoid(z[:, Z_BETA:Z_COLS])
    g_cum = _seg_scan_sum(g_log, chunk, row % chunk)
    g_end = jnp.concatenate(
        [jnp.broadcast_to(g_cum[(i + 1) * chunk - 1:(i + 1) * chunk], (chunk, W_GROUP)) for i in range(rows // chunk)],
        axis=0)
    gsum = gavg * 64.0
    qn, kn, vv = yq[:, 0:256], yq[:, 256:512], yq[:, 512:768]
    qn = qn * lax.rsqrt(_dot(qn * qn, gsum) + EPS) * (A_DK ** -0.5)
    kn = kn * lax.rsqrt(_dot(kn * kn, gsum) + EPS)
    e_cum = jnp.exp(g_cum)
    rhs_v = vv * beta
    k_beta = kn * beta
    rhs_k = k_beta * e_cum
    q_dec_full = qn * e_cum
    k_dec_full = kn * jnp.exp(g_end - g_cum)

    ri = lax.broadcasted_iota(jnp.int32, (STACK, STACK), 0)
    ci = lax.broadcasted_iota(jnp.int32, (STACK, STACK), 1)
    same = (ri // chunk) == (ci // chunk)
    causal = same & (ri >= ci)
    strict = same & (ri > ci)
    erow = lax.broadcasted_iota(jnp.int32, (STACK, groups * 64), 0)
    ecol = lax.broadcasted_iota(jnp.int32, (STACK, groups * 64), 1)
    expand_mask = (erow // chunk) == (ecol // 64)
    trow = lax.broadcasted_iota(jnp.int32, (groups * 64, STACK), 0)
    tcol = lax.broadcasted_iota(jnp.int32, (groups * 64, STACK), 1)
    expand_mask_t = (trow // 64) == (tcol // chunk)
    eye64 = (lax.broadcasted_iota(jnp.int32, (64, 64), 0)
             == lax.broadcasted_iota(jnp.int32, (64, 64), 1)).astype(BF16)

    def stack_heads(x):
        return jnp.concatenate([x[:, h * 64:(h + 1) * 64] for h in range(A_HEADS)], axis=0)

    def expand(x):
        return jnp.where(expand_mask, jnp.concatenate([x] * groups, axis=1), 0.0)

    def transpose64(x):
        return lax.dot_general(eye64, x.astype(BF16), (((1,), (1,)), ((), ())), preferred_element_type=F32)

    cs = range(nchunks)
    sls = [slice(c * 64, (c + 1) * 64) for c in cs]
    q = [stack_heads(qn[sl]) for sl in sls]
    k = [stack_heads(kn[sl]) for sl in sls]
    g_s = [stack_heads(g_cum[sl]) for sl in sls]
    g_c = [jnp.concatenate([x] * A_HEADS, axis=1) for x in g_s]
    g_r = [x.T for x in g_c]
    decay = [jnp.exp(jnp.where(causal, g_c[c] - g_r[c], NEG)) for c in cs]
    akq = [_dot_nt(jnp.concatenate([stack_heads(k_beta[sls[c]]), q[c]], axis=0), k[c]) for c in cs]
    amat = [jnp.where(strict, akq[c][:STACK] * decay[c], 0.0) for c in cs]
    qk = [akq[c][STACK:] * decay[c] for c in cs]

    def compress(x):
        return functools.reduce(lambda u, w: u + w, [x[g * chunk:(g + 1) * chunk] for g in range(groups)])

    def block_diag(xc):
        return jnp.where(same, jnp.concatenate([xc] * groups, axis=0), 0.0)

    eye_c = (lax.broadcasted_iota(jnp.int32, (chunk, STACK), 0)
             == lax.broadcasted_iota(jnp.int32, (chunk, STACK), 1) % chunk).astype(F32)
    x_inv = [eye_c - compress(jnp.where((ri // 2) == (ci // 2), amat[c], 0.0)) for c in cs]
    blk = 2
    while blk < chunk:
        merge = ((ri // (2 * blk)) == (ci // (2 * blk))) & ((ri // blk) != (ci // blk))
        left = [_dot(x_inv[c], jnp.where(merge, amat[c], 0.0)) for c in cs]
        if side_work:
            side_work.pop(0)()
        x_inv = [x_inv[c] - _dot(left[c], block_diag(x_inv[c])) for c in cs]
        blk *= 2
    while side_work:
        side_work.pop(0)()
    sol = [_dot(block_diag(x_inv[c]),
                jnp.concatenate([stack_heads(rhs_v[sls[c]]), stack_heads(rhs_k[sls[c]])], axis=1)) for c in cs]
    w_val = [x[:, :64] for x in sol]
    w_key = [x[:, 64:] for x in sol]
    q_dec = [stack_heads(q_dec_full[sl]) for sl in sls]
    kd_t = [transpose64(stack_heads(k_dec_full[sl])) for sl in sls]
    kd_t_exp = [jnp.where(expand_mask_t, jnp.concatenate([x] * groups, axis=0), 0.0) for x in kd_t]
    g_end_s = [stack_heads(g_end[sl]) for sl in sls]
    s_decay = [jnp.exp(jnp.concatenate(
        [jnp.broadcast_to(x[g * chunk:g * chunk + 1], (64, 64)) for g in range(groups)], axis=0)) for x in g_end_s]

    state = s_scr[...]
    o_chunks = []
    if nchunks == 1:
        s_b = state.astype(BF16)
        u = w_val[0] - _dot(expand(w_key[0]), s_b)
        o_chunks.append(_dot(expand(q_dec[0]), s_b) + _dot(qk[0], u))
        state = state * s_decay[0] + _dot_nt(kd_t_exp[0], transpose64(u))
    else:
        qk_sol = [_dot(qk[c], sol[c]) for c in cs]
        lhs = [jnp.concatenate([expand(q_dec[c] - qk_sol[c][:, 64:]), _dot(kd_t_exp[c], expand(w_key[c]))], axis=0)
               for c in cs]
        s_add = [_dot(kd_t_exp[c], w_val[c]) for c in cs]
        for c in cs:
            from_state = _dot(lhs[c], state.astype(BF16))
            o_chunks.append(from_state[:STACK] + qk_sol[c][:, :64])
            state = state * s_decay[c] - from_state[STACK:] + s_add[c]
    s_scr[...] = state
    oa = [jnp.concatenate([o[h * 64:(h + 1) * 64] for h in range(A_HEADS)], axis=1) for o in o_chunks]
    oa = oa[0] if nchunks == 1 else jnp.concatenate(oa, axis=0)
    oa = oa * lax.rsqrt(_dot(oa * oa, gavg) + EPS) * onorm_ref[...] * _silu(z[:, Z_GATE:Z_B])

    mix = jnp.concatenate([oa, results["ob"], results["oc"], results["od"]], axis=1)
    x1 = x_in + _rms(_dot(mix, wout_ref[...]), gpost_ref[...])
    x1_ref[...] = x1
    q_ref[...] = _dot(_rms(x1, gpre_ref[...]), wq_ref[...]).astype(q_ref.dtype)

    @pl.when(t == nt - 1)
    def _store_state():
        for s in range(nseg):
            dconv_o[s] = xq_scr[s, qoff - (A_CONV - 1):qoff, :]
            bconv_o[s] = xb_scr[s, boff - (B_CONV - 1):boff, :]
            pbuf_o[s] = xc_scr[s, coff - POOL_BUF:coff, :]
            lconv_o[s] = xd_scr[s, doff - (D_CONV - 1):doff, :]
            lh_o[s] = h_scr[s]
            for h in range(A_HEADS):
                g = h * seq_per_chunk + s
                ds_o[s, h] = s_scr[g * 64:(g + 1) * 64, :]


def _mixer(x, state_in, layer_in, state_out, layer_out, lw, *, nseg, tn, chunk, pos0):
    bsz, tlen, d = x.shape
    rows = nseg * tn
    assert rows % 64 == 0 and (nseg == 1 or rows == 64) and tlen % tn == 0 and bsz % nseg == 0
    assert tn % SUBLANES == 0
    assert (nseg == 1 and chunk == 64) or (nseg > 1 and chunk == tn and tlen == tn)
    groups = STACK // chunk
    grid = (bsz // nseg, tlen // tn)

    def state_spec(a, layer):
        blk = (None, nseg) + a.shape[2:]
        nd = a.ndim
        return pl.BlockSpec(blk, lambda b, t: (layer, b) + (0,) * (nd - 2))

    weights = (lw["norm_mix_pre"], lw["w_in"], lw["conv_qkv"], lw["hp"], lw["onorm"], lw["dw_b"], lw["brow"],
               lw["w_pw"], lw["w_pool"], lw["conv_d"], lw["w_rg"], lw["w_ig"], lw["gavg"],
               lw["w_out"], lw["norm_mix_post"], lw["norm_x_pre"], lw["w_xq"])
    n_in = 1 + len(state_in) + len(weights)
    nt = tlen // tn
    row_spec = pl.BlockSpec((rows, d), lambda b, t: (b * nt + t, 0))
    out_shape = (jax.ShapeDtypeStruct((bsz * tlen, d), F32), jax.ShapeDtypeStruct((bsz * tlen, d), BF16)) + tuple(
        jax.ShapeDtypeStruct(a.shape, a.dtype) for a in state_out)
    kern = functools.partial(_mixer_kernel, nseg=nseg, tn=tn, chunk=chunk, pos0=pos0)
    return pl.pallas_call(
        kern,
        grid=grid,
        in_specs=[pl.BlockSpec((nseg, tn, d), lambda b, t: (b, t, 0))]
        + [state_spec(a, layer_in) for a in state_in] + [_layer_spec(w, layer_out) for w in weights]
        + [pl.BlockSpec(memory_space=pl.ANY) for _ in state_out],
        out_specs=[row_spec, row_spec] + [state_spec(a, layer_out) for a in state_out],
        out_shape=out_shape,
        input_output_aliases={n_in + j: 2 + j for j in range(len(state_out))},
        scratch_shapes=[
            pltpu.VMEM((nseg, 8 + tn + SUBLANES, A_QKV), F32),
            pltpu.VMEM((nseg, 32 + tn + SUBLANES, W_GROUP), F32),
            pltpu.VMEM((nseg, 16 + tn, W_GROUP), F32),
            pltpu.VMEM((nseg, 8 + tn + SUBLANES, W_GROUP), F32),
            pltpu.VMEM((groups * 64, 64), F32),
            pltpu.VMEM((nseg, 1, W_GROUP), F32),
        ],
        compiler_params=_cparams("parallel", "arbitrary"),
        name="mixer",
    )(x, *state_in, *weights, *state_out)


def _softmax_rows(s):
    e = jnp.exp(s - jnp.max(s, axis=-1, keepdims=True))
    return e / jnp.sum(e, axis=-1, keepdims=True)


def _attn_kernel(q_ref, k_ref, v_ref, o_ref):
    q = q_ref[0]
    heads = [slice(h * X_HD, (h + 1) * X_HD) for h in range(X_HEADS)]
    s = [_dot_nt(q[:, sl], k_ref[:, sl]) * (X_HD ** -0.5) for sl in heads]
    p = [_softmax_rows(x) for x in s]
    o = [_dot(p[h], v_ref[:, heads[h]]) for h in range(X_HEADS)]
    o_ref[0] = jnp.concatenate(o, axis=1).astype(o_ref.dtype)


def _attn(q, k_rows, v_rows, layer, tq):
    bsz, tlen, d = q.shape
    tile = lambda b, t: (b, t, 0)
    mem_spec = pl.BlockSpec((None, None, N_MEM, d), lambda b, t: (layer, b, 0, 0))
    return pl.pallas_call(
        _attn_kernel,
        grid=(bsz, tlen // tq),
        in_specs=[pl.BlockSpec((1, tq, d), tile), mem_spec, mem_spec],
        out_specs=pl.BlockSpec((1, tq, d), tile),
        out_shape=jax.ShapeDtypeStruct((bsz, tlen, d), BF16),
        compiler_params=_cparams("parallel", "arbitrary"),
        name="attn",
    )(q, k_rows, v_rows)


def _attn_cache_kernel(q_ref, k_ref, v_ref, o_ref, *, nseq, tlen):
    hrows = X_HEADS * tlen
    qrow_head = lax.broadcasted_iota(jnp.int32, (hrows, N_MEM * X_HEADS), 0) // tlen
    col_head = lax.broadcasted_iota(jnp.int32, (hrows, N_MEM * X_HEADS), 1) % X_HEADS
    own = qrow_head == col_head
    seqs = range(nseq)
    qs = [q_ref[s].astype(F32) for s in seqs]
    q4 = [jnp.concatenate([x[:, h * X_HD:(h + 1) * X_HD] for h in range(X_HEADS)], axis=0) for x in qs]
    sc = [_dot_nt(q4[s], k_ref[0, s].reshape(N_MEM * X_HEADS, X_HD)) * (X_HD ** -0.5) for s in seqs]
    p = [_softmax_rows(jnp.where(own, x, NEG)) for x in sc]
    o4 = [_dot(p[s], v_ref[0, s].reshape(N_MEM * X_HEADS, X_HD)) for s in seqs]
    for s in seqs:
        o_ref[s] = jnp.concatenate([o4[s][h * tlen:(h + 1) * tlen] for h in range(X_HEADS)], axis=1)


def _attn_cache(q, cache_k, cache_v, layer, nseq):
    bsz, tlen, d = q.shape
    cache_blk = (1, nseq) + cache_k.shape[2:]
    cache_map = lambda i: (layer, i, 0, 0, 0)
    kern = functools.partial(_attn_cache_kernel, nseq=nseq, tlen=tlen)
    return pl.pallas_call(
        kern,
        grid=(bsz // nseq,),
        in_specs=[pl.BlockSpec((nseq, tlen, d), lambda i: (i, 0, 0)), pl.BlockSpec(cache_blk, cache_map),
                  pl.BlockSpec(cache_blk, cache_map)],
        out_specs=pl.BlockSpec((nseq, tlen, d), lambda i: (i, 0, 0)),
        out_shape=jax.ShapeDtypeStruct((bsz, tlen, d), F32),
        compiler_params=_cparams("parallel"),
        name="attn_cache",
    )(q, cache_k, cache_v)


def _xo_ffn_kernel(o_ref, x_ref, wo_ref, gxo_ref, gpre_ref, win_ref, wout_ref, gpost_ref, y_ref):
    x = x_ref[...] + _rms(_dot(o_ref[...], wo_ref[...]), gxo_ref[...])
    gu = _dot(_rms(x, gpre_ref[...]), win_ref[...])
    act = _silu(gu[:, :FFN]) * gu[:, FFN:]
    y_ref[...] = x + _rms(_dot(act, wout_ref[...]), gpost_ref[...])


def _xo_ffn(o, x, lw, layer, tm):
    m, d = x.shape
    tm = min(tm, m)
    assert m % tm == 0
    row = lambda i: (i, 0)
    weights = (lw["w_xo"], lw["norm_x_post"], lw["norm_ffn_pre"], lw["w_ffn_in"], lw["w_ffn_out"],
               lw["norm_ffn_post"])
    return pl.pallas_call(
        _xo_ffn_kernel,
        grid=(m // tm,),
        in_specs=[pl.BlockSpec((tm, d), row), pl.BlockSpec((tm, d), row)] + [_layer_spec(w, layer) for w in weights],
        out_specs=pl.BlockSpec((tm, d), row),
        out_shape=jax.ShapeDtypeStruct((m, d), F32),
        compiler_params=_cparams("parallel"),
        name="xo_ffn",
    )(o, x, *weights)


def _block_diag(w):
    depth, n, k, _ = w.shape
    return (w[:, :, :, None, :] * jnp.eye(n, dtype=w.dtype)[None, :, None, :, None]).reshape(depth, n * k, n * k)


def _prep_weights(p):
    w_in = p["w_in"]
    depth = w_in.shape[0]
    off_araw, off_gate = A_QKV, A_QKV + 2 * A_HEADS
    per_head = lambda a: jnp.repeat(a, 64, axis=-1)
    w_in = jnp.concatenate(
        [w_in[..., :off_araw], w_in[..., off_gate:], per_head(w_in[..., off_araw:off_araw + A_HEADS]),
         per_head(w_in[..., off_araw + A_HEADS:off_gate])], axis=-1)
    row = lambda a: a[:, None, :]
    gavg = _block_diag(jnp.full((depth, B_GROUPS, 64, 64), 1.0 / 64, F32))
    return {
        "w_in": w_in.astype(BF16),
        "norm_mix_pre": row(p["norm_mix_pre"]),
        "conv_qkv": p["conv_qkv"],
        "hp": jnp.stack([per_head(p["a_log"]), per_head(p["dt_bias"])], axis=1),
        "onorm": row(jnp.tile(p["onorm_a"], (1, A_HEADS))),
        "dw_b": p["dw_b"],
        "brow": jnp.stack([p["dwbias_b"], p["gn_gain_b"], p["gn_bias_b"], p["scale_pool"],
                           p["conv_bias_d"], p["b_rg"], p["b_ig"], p["lam_d"]], axis=1),
        "w_pw": p["w_pw_b"].astype(BF16),
        "w_pool": _block_diag(p["w_pool"]).astype(BF16),
        "conv_d": p["conv_d"],
        "w_rg": _block_diag(p["w_rg"]).astype(BF16),
        "w_ig": _block_diag(p["w_ig"]).astype(BF16),
        "gavg": gavg.astype(BF16),
        "w_out": p["w_out"].astype(BF16),
        "norm_mix_post": row(p["norm_mix_post"]),
        "norm_x_pre": row(p["norm_x_pre"]),
        "w_xq": p["w_xq"].astype(BF16),
        "w_xo": p["w_xo"].astype(BF16),
        "norm_x_post": row(p["norm_x_post"]),
        "norm_ffn_pre": row(p["norm_ffn_pre"]),
        "w_ffn_in": p["w_ffn_in"].astype(BF16),
        "w_ffn_out": p["w_ffn_out"].astype(BF16),
        "norm_ffn_post": row(p["norm_ffn_post"]),
    }


def _layer(x, attend, state_in, layer_in, state_out, layer, lw, *, nseg, tn, chunk, pos0, tm):
    bsz, tlen, d = x.shape
    x1, q, *new_state = _mixer(x, state_in, layer_in, state_out, layer, lw,
                               nseg=nseg, tn=tn, chunk=chunk, pos0=pos0)
    o = attend(q.reshape(bsz, tlen, d))
    x3 = _xo_ffn(o.reshape(bsz * tlen, d), x1, lw, layer, tm)
    return x3.reshape(bsz, tlen, d), new_state


def kernel(x_prompt, x_sample, mem_prompt, state_delta, state_delta_conv, state_conf_conv, state_pool,
           state_lru_conv, state_lru_h, cache_mem_k, cache_mem_v, norm_mix_pre, norm_mix_post, w_in, conv_qkv,
           a_log, dt_bias, onorm_a, dw_b, dwbias_b, gn_gain_b, gn_bias_b, w_pw_b, w_pool, scale_pool, conv_d,
           conv_bias_d, w_rg, b_rg, w_ig, b_ig, lam_d, w_out, norm_x_pre, norm_x_post, norm_mem, w_xq, w_xkv,
           w_xo, norm_ffn_pre, norm_ffn_post, w_ffn_in, w_ffn_out):
    depth = w_in.shape[0]
    bp, tp, d = x_prompt.shape
    bs, ts, _ = x_sample.shape
    dt = x_prompt.dtype
    per_layer = dict(norm_mix_pre=norm_mix_pre, norm_mix_post=norm_mix_post, w_in=w_in, conv_qkv=conv_qkv,
                     a_log=a_log, dt_bias=dt_bias, onorm_a=onorm_a, dw_b=dw_b, dwbias_b=dwbias_b,
                     gn_gain_b=gn_gain_b, gn_bias_b=gn_bias_b, w_pw_b=w_pw_b, w_pool=w_pool,
                     scale_pool=scale_pool, conv_d=conv_d, conv_bias_d=conv_bias_d, w_rg=w_rg, b_rg=b_rg,
                     w_ig=w_ig, b_ig=b_ig, lam_d=lam_d, w_out=w_out, norm_x_pre=norm_x_pre,
                     norm_x_post=norm_x_post, w_xq=w_xq, w_xo=w_xo, norm_ffn_pre=norm_ffn_pre,
                     norm_ffn_post=norm_ffn_post, w_ffn_in=w_ffn_in, w_ffn_out=w_ffn_out)
    mem_k_p, mem_v_p, k_rows, v_rows = _memory_kv(mem_prompt, norm_mem[:, None, :], w_xkv.astype(BF16))

    def state_shapes(b):
        return ((b, A_CONV - 1, A_QKV), (b, A_HEADS, A_DK, A_DK), (b, B_CONV - 1, W_GROUP), (b, POOL_BUF, W_GROUP),
                (b, D_CONV - 1, W_GROUP), (b, 1, W_GROUP))

    zero_state = tuple(jnp.zeros((1,) + s, dt) for s in state_shapes(bp))
    sample_state = (state_delta_conv, state_delta, state_conf_conv, state_pool, state_lru_conv,
                    state_lru_h[:, :, None, :])
    st_p = tuple(jnp.zeros((depth,) + s, dt) for s in state_shapes(bp))
    st_s = tuple(jnp.zeros((depth,) + s, dt) for s in state_shapes(bs))
    xp, xs = x_prompt, x_sample
    tn_p = min(tp, 512)
    lw = _prep_weights(per_layer)
    for l in range(depth):
        xp, st_p = _layer(xp, lambda q: _attn(q, k_rows, v_rows, l, min(tp, 512)), zero_state, 0, st_p, l, lw,
                          nseg=1, tn=tn_p, chunk=64, pos0=0, tm=512)
        xs, st_s = _layer(xs, lambda q: _attn_cache(q, cache_mem_k, cache_mem_v, l, 4), sample_state, l, st_s, l, lw,
                          nseg=64 // ts, tn=ts, chunk=ts, pos0=PAST_LEN, tm=512)
    dconv_p, delta_p, conf_p, pool_p, lconv_p, lh_p = st_p
    dconv_s, delta_s, conf_s, pool_s, lconv_s, lh_s = st_s
    return (xp, xs, delta_p, delta_s, dconv_p, dconv_s, conf_p, conf_s, pool_p, pool_s,
            lconv_p, lconv_s, lh_p[:, :, 0, :], lh_s[:, :, 0, :], mem_k_p, mem_v_p)
```

```python
import functools

import jax
import jax.numpy as jnp
from jax import lax
from jax.experimental import pallas as pl
from jax.experimental.pallas import tpu as pltpu

F32 = jnp.float32
BF16 = jnp.bfloat16
EPS = 1e-6

D_MODEL = 1024
W_GROUP = 256
A_HEADS = 4
A_DK = 64
A_CONV = 4
A_QKV = 768
B_CONV = 31
B_GROUPS = 4
POOL_BUF = 15
D_CONV = 4
LRU_C = 8.0
N_MEM = 256
X_HEADS = 4
X_HD = 256
FFN = 2816
PAST_LEN = 16384

Z_QKV = 0
Z_GATE = 768
Z_B = 1024
Z_C = 1536
Z_D = 1792
Z_G = 2304
Z_BETA = 2560
Z_COLS = 2816

SUBLANES = 8
STACK = A_HEADS * 64
NEG = -1e30
VMEM_LIMIT = 56 * 1024 * 1024


def _cparams(*sem):
    return pltpu.CompilerParams(dimension_semantics=sem, vmem_limit_bytes=VMEM_LIMIT)


def _layer_spec(a, layer):
    nd = a.ndim
    return pl.BlockSpec((None,) + a.shape[1:], lambda *_: (layer,) + (0,) * (nd - 1),
                        pipeline_mode=pl.Buffered(1))


def _rms(x, g):
    return x * lax.rsqrt(jnp.mean(x * x, axis=-1, keepdims=True) + EPS) * g


def _sigmoid(x):
    return 0.5 * jnp.tanh(0.5 * x) + 0.5


def _silu(x):
    return x * _sigmoid(x)


def _softplus(x):
    return jnp.maximum(x, 0.0) + jnp.log(1.0 + jnp.exp(-jnp.abs(x)))


def _dot(a, b):
    return jnp.dot(a.astype(BF16), b.astype(BF16), preferred_element_type=F32)


def _dot_nt(a, b):
    return lax.dot_general(a.astype(BF16), b.astype(BF16), (((1,), (1,)), ((), ())),
                           preferred_element_type=F32)


def _dot_2pass(x, w):
    hi = x.astype(BF16)
    lo = (x - hi.astype(F32)).astype(BF16)
    return (jnp.dot(hi, w, preferred_element_type=F32) + jnp.dot(lo, w, preferred_element_type=F32))


def _memory_kv_kernel(mem_ref, g_ref, w_ref, k_ref, v_ref, k_rows_ref, v_rows_ref):
    kv = _dot(_rms(mem_ref[0], g_ref[...]), w_ref[...])
    k, v = kv[:, :D_MODEL], kv[:, D_MODEL:]
    k_ref[...] = k.reshape(N_MEM, X_HEADS, X_HD)
    v_ref[...] = v.reshape(N_MEM, X_HEADS, X_HD)
    k_rows_ref[...] = k.astype(k_rows_ref.dtype)
    v_rows_ref[...] = v.astype(v_rows_ref.dtype)


def _memory_kv(mem, g, w):
    bsz, n_mem, d = mem.shape
    depth = w.shape[0]
    cache_spec = pl.BlockSpec((None, None, n_mem, X_HEADS, X_HD), lambda l, b: (l, b, 0, 0, 0))
    rows_spec = pl.BlockSpec((None, None, n_mem, d), lambda l, b: (l, b, 0, 0))
    cache_shape = jax.ShapeDtypeStruct((depth, bsz, n_mem, X_HEADS, X_HD), F32)
    rows_shape = jax.ShapeDtypeStruct((depth, bsz, n_mem, d), BF16)
    return pl.pallas_call(
        _memory_kv_kernel,
        grid=(depth, bsz),
        in_specs=[pl.BlockSpec((1, n_mem, d), lambda l, b: (b, 0, 0)),
                  pl.BlockSpec((None, 1, d), lambda l, b: (l, 0, 0)),
                  pl.BlockSpec((None, d, 2 * d), lambda l, b: (l, 0, 0))],
        out_specs=[cache_spec, cache_spec, rows_spec, rows_spec],
        out_shape=[cache_shape, cache_shape, rows_shape, rows_shape],
        compiler_params=_cparams("parallel", "arbitrary"),
        name="memory_kv",
    )(mem, g, w)


def _seg_scan_sum(x, seg_len, row_in_seg):
    s = 1
    while s < seg_len:
        x = x + jnp.where(row_in_seg >= s, pltpu.roll(x, s, axis=0), 0.0)
        s *= 2
    return x


def _linear_scan(a, b, carries, seg_rows):
    rows, width = a.shape
    nblk = rows // SUBLANES
    a = a.reshape(nblk, SUBLANES, width)
    b = b.reshape(nblk, SUBLANES, width)
    sub = lax.broadcasted_iota(jnp.int32, (1, SUBLANES, 1), 1)
    s = 1
    while s < SUBLANES:
        keep = sub >= s
        a_sh = jnp.where(keep, pltpu.roll(a, s, axis=1), 1.0)
        b_sh = jnp.where(keep, pltpu.roll(b, s, axis=1), 0.0)
        b = a * b_sh + b
        a = a * a_sh
        s *= 2
    blk_per_seg = seg_rows // SUBLANES
    outs = []
    carry = None
    for g in range(nblk):
        if g % blk_per_seg == 0:
            carry = carries[g // blk_per_seg]
        hg = a[g] * carry + b[g]
        carry = hg[SUBLANES - 1:SUBLANES]
        outs.append(hg)
    return jnp.concatenate(outs, axis=0)


def _conv_segment(scr, seg, x, w_rows, width, off, tn, row_block):
    scr[seg, off:off + tn, :] = x
    base = off - (width - 1)
    outs = []
    for r0 in range(0, tn, row_block):
        rb = min(row_block, tn - r0)
        acc = None
        for res in range(SUBLANES):
            part = None
            for p in range(base, base + width):
                if p % SUBLANES != res:
                    continue
                lo = p - res + r0
                term = scr[seg, lo:lo + rb + SUBLANES, :] * w_rows[p - base]
                part = term if part is None else part + term
            if part is None:
                continue
            piece = part[res:res + rb]
            acc = piece if acc is None else acc + piece
        outs.append(acc)
    tail = scr[seg, off + tn - (width - 1):off + tn, :]
    scr[seg, off - (width - 1):off, :] = tail
    return outs[0] if len(outs) == 1 else jnp.concatenate(outs, axis=0)


def _pool_segment(scr, seg, u, off, tn):
    scr[seg, off:off + tn, :] = u
    x = scr[seg, 0:off + tn, :]
    s2 = x + pltpu.roll(x, 1, axis=0)
    s4 = s2 + pltpu.roll(s2, 2, axis=0)
    s8 = s4 + pltpu.roll(s4, 4, axis=0)
    s16 = s8[off:] + s8[off - 8:off - 8 + tn]
    lane = lax.broadcasted_iota(jnp.int32, (1, W_GROUP), 1)
    wsum = jnp.where(lane < 64, s2[off:], jnp.where(lane < 128, s4[off:], jnp.where(lane < 192, s8[off:], s16)))
    tail = scr[seg, off + tn - POOL_BUF:off + tn, :]
    scr[seg, off - POOL_BUF:off, :] = tail
    return wsum


def _mixer_kernel(x_ref, dconv_ref, ds_ref, bconv_ref, pbuf_ref, lconv_ref, lh_ref,
                  gin_ref, win_ref, wqkv_ref, hp_ref, onorm_ref, dw_ref, brow_ref, wpw_ref, wpool_ref, convd_ref,
                  wrg_ref, wig_ref, gavg_ref, wout_ref, gpost_ref, gpre_ref, wq_ref,
                  dconv_all, ds_all, bconv_all, pbuf_all, lconv_all, lh_all,
                  x1_ref, q_ref, dconv_o, ds_o, bconv_o, pbuf_o, lconv_o, lh_o,
                  xq_scr, xb_scr, xc_scr, xd_scr, s_scr, h_scr,
                  *, nseg, tn, chunk, pos0):
    del dconv_all, ds_all, bconv_all, pbuf_all, lconv_all, lh_all
    t = pl.program_id(1)
    nt = pl.num_programs(1)
    rows = nseg * tn
    nchunks = rows // 64
    groups = STACK // chunk
    seq_per_chunk = 64 // chunk
    qoff, boff, coff, doff = 8, 32, 16, 8

    @pl.when(t == 0)
    def _load_state():
        for s in range(nseg):
            xq_scr[s, 0:qoff, :] = jnp.zeros((qoff, A_QKV), F32)
            xq_scr[s, qoff + tn:, :] = jnp.zeros((SUBLANES, A_QKV), F32)
            xb_scr[s, 0:SUBLANES, :] = jnp.zeros((SUBLANES, W_GROUP), F32)
            xb_scr[s, boff + tn:, :] = jnp.zeros((SUBLANES, W_GROUP), F32)
            xc_scr[s, 0:SUBLANES, :] = jnp.zeros((SUBLANES, W_GROUP), F32)
            xd_scr[s, 0:doff, :] = jnp.zeros((doff, W_GROUP), F32)
            xd_scr[s, doff + tn:, :] = jnp.zeros((SUBLANES, W_GROUP), F32)
            xq_scr[s, qoff - (A_CONV - 1):qoff, :] = dconv_ref[s]
            xb_scr[s, boff - (B_CONV - 1):boff, :] = bconv_ref[s]
            xc_scr[s, coff - POOL_BUF:coff, :] = pbuf_ref[s]
            xd_scr[s, doff - (D_CONV - 1):doff, :] = lconv_ref[s]
            h_scr[s] = lh_ref[s]
            for h in range(A_HEADS):
                c, sc = divmod(s, seq_per_chunk)
                g = h * seq_per_chunk + sc
                s_scr[c, g * 64:(g + 1) * 64, :] = ds_ref[s, h]

    x_in = x_ref[0] if nseg == 1 else x_ref[...].reshape(rows, D_MODEL)
    z = _dot(_rms(x_in, gin_ref[...]), win_ref[...])

    def per_segment(fn, x):
        outs = [fn(s, x[s * tn:(s + 1) * tn]) for s in range(nseg)]
        return outs[0] if nseg == 1 else jnp.concatenate(outs, axis=0)

    row = lax.broadcasted_iota(jnp.int32, (rows, 1), 0)
    row_in_seg = row % tn
    brow = brow_ref[...]
    gavg = gavg_ref[...]
    results = {}

    def section_b_conv():
        ub = z[:, Z_B:Z_C]
        glu = ub[:, :W_GROUP] * _sigmoid(ub[:, W_GROUP:])
        dw_rows = [dw_ref[j:j + 1, :] for j in range(B_CONV)]
        yb = per_segment(lambda s, x: _conv_segment(xb_scr, s, x, dw_rows, B_CONV, boff, tn, 64), glu)
        results["yb"] = yb + brow[0:1]

    def section_b_norm():
        yb = results["yb"]
        yc = yb - _dot_2pass(yb, gavg)
        var = _dot(yc * yc, gavg)
        yn = yc * lax.rsqrt(var + EPS) * brow[1:2] + brow[2:3]
        results["ob"] = _dot(_silu(yn), wpw_ref[...])

    def section_c():
        uc = z[:, Z_C:Z_D]
        wsum = per_segment(lambda s, x: _pool_segment(xc_scr, s, x, coff, tn), uc)
        lane = lax.broadcasted_iota(jnp.int32, (1, W_GROUP), 1)
        wlen = jnp.where(lane < 64, 2.0, jnp.where(lane < 128, 4.0, jnp.where(lane < 192, 8.0, 16.0)))
        pos = (pos0 + t * tn + row_in_seg).astype(F32)
        cnt = jnp.minimum(pos + 1.0, wlen)
        results["oc"] = _dot(wsum / cnt - uc, wpool_ref[...]) * brow[3:4]

    def section_d_gates():
        ud = z[:, Z_D:Z_G]
        results["gate_d"] = jax.nn.gelu(ud[:, :W_GROUP])
        cd_rows = [convd_ref[j:j + 1, :] for j in range(D_CONV)]
        xr = per_segment(lambda s, x: _conv_segment(xd_scr, s, x, cd_rows, D_CONV, doff, tn, 64), ud[:, W_GROUP:])
        xr = xr + brow[4:5]
        r_gate = _sigmoid(_dot(xr, wrg_ref[...]) + brow[5:6])
        i_gate = _sigmoid(_dot(xr, wig_ref[...]) + brow[6:7])
        log_a = -LRU_C * r_gate * _softplus(-brow[7:8])
        a = jnp.exp(log_a)
        results["lru_a"] = a
        results["lru_b"] = jnp.sqrt(-jnp.tanh(log_a) * (a * a + 1.0)) * (i_gate * xr)

    def section_d_scan():
        hseq = _linear_scan(results["lru_a"], results["lru_b"], [h_scr[s] for s in range(nseg)], tn)
        results["od"] = results["gate_d"] * hseq
        for s in range(nseg):
            h_scr[s] = hseq[(s + 1) * tn - 1:(s + 1) * tn, :]

    side_work = [section_b_conv, section_b_norm, section_c, section_d_gates, section_d_scan]

    wq_rows = [wqkv_ref[j:j + 1, :] for j in range(A_CONV)]
    yq = per_segment(lambda s, x: _conv_segment(xq_scr, s, x, wq_rows, A_CONV, qoff, tn, 64), z[:, Z_QKV:Z_GATE])
    yq = _silu(yq)
    hp = hp_ref[...]
    g_log = -jnp.exp(hp[0:1]) * _softplus(z[:, Z_G:Z_BETA] + hp[1:2])
    beta = _sigmoid(z[:, Z_BETA:Z_COLS])
    g_cum = _seg_scan_sum(g_log, chunk, row % chunk)
    g_end = jnp.concatenate(
        [jnp.broadcast_to(g_cum[(i + 1) * chunk - 1:(i + 1) * chunk], (chunk, W_GROUP)) for i in range(rows // chunk)],
        axis=0)
    gsum = gavg * 64.0
    qn, kn, vv = yq[:, 0:256], yq[:, 256:512], yq[:, 512:768]
    qn = qn * lax.rsqrt(_dot(qn * qn, gsum) + EPS) * (A_DK ** -0.5)
    kn = kn * lax.rsqrt(_dot(kn * kn, gsum) + EPS)
    e_cum = jnp.exp(g_cum)
    rhs_v = vv * beta
    k_beta = kn * beta
    rhs_k = k_beta * e_cum
    q_dec_full = qn * e_cum
    k_dec_full = kn * jnp.exp(g_end - g_cum)

    ri = lax.broadcasted_iota(jnp.int32, (STACK, STACK), 0)
    ci = lax.broadcasted_iota(jnp.int32, (STACK, STACK), 1)
    same = (ri // chunk) == (ci // chunk)
    causal = same & (ri >= ci)
    strict = same & (ri > ci)
    erow = lax.broadcasted_iota(jnp.int32, (STACK, groups * 64), 0)
    ecol = lax.broadcasted_iota(jnp.int32, (STACK, groups * 64), 1)
    expand_mask = (erow // chunk) == (ecol // 64)
    trow = lax.broadcasted_iota(jnp.int32, (groups * 64, STACK), 0)
    tcol = lax.broadcasted_iota(jnp.int32, (groups * 64, STACK), 1)
    expand_mask_t = (trow // 64) == (tcol // chunk)
    eye64 = (lax.broadcasted_iota(jnp.int32, (64, 64), 0)
             == lax.broadcasted_iota(jnp.int32, (64, 64), 1)).astype(BF16)

    def stack_heads(x):
        return jnp.concatenate([x[:, h * 64:(h + 1) * 64] for h in range(A_HEADS)], axis=0)

    def expand(x):
        return jnp.where(expand_mask, jnp.concatenate([x] * groups, axis=1), 0.0)

    def transpose64(x):
        return lax.dot_general(eye64, x.astype(BF16), (((1,), (1,)), ((), ())), preferred_element_type=F32)

    cs = range(nchunks)
    sls = [slice(c * 64, (c + 1) * 64) for c in cs]
    q = [stack_heads(qn[sl]) for sl in sls]
    k = [stack_heads(kn[sl]) for sl in sls]
    g_s = [stack_heads(g_cum[sl]) for sl in sls]
    g_c = [jnp.concatenate([x] * A_HEADS, axis=1) for x in g_s]
    g_r = [x.T for x in g_c]
    decay = [jnp.exp(jnp.where(causal, g_c[c] - g_r[c], NEG)) for c in cs]
    akq = [_dot_nt(jnp.concatenate([stack_heads(k_beta[sls[c]]), q[c]], axis=0), k[c]) for c in cs]
    amat = [jnp.where(strict, akq[c][:STACK] * decay[c], 0.0) for c in cs]
    qk = [akq[c][STACK:] * decay[c] for c in cs]

    def compress(x):
        return functools.reduce(lambda u, w: u + w, [x[g * chunk:(g + 1) * chunk] for g in range(groups)])

    def block_diag(xc):
        return jnp.where(same, jnp.concatenate([xc] * groups, axis=0), 0.0)

    eye_c = (lax.broadcasted_iota(jnp.int32, (chunk, STACK), 0)
             == lax.broadcasted_iota(jnp.int32, (chunk, STACK), 1) % chunk).astype(F32)
    x_inv = [eye_c - compress(jnp.where((ri // 2) == (ci // 2), amat[c], 0.0)) for c in cs]
    blk = 2
    while blk < chunk:
        merge = ((ri // (2 * blk)) == (ci // (2 * blk))) & ((ri // blk) != (ci // blk))
        left = [_dot(x_inv[c], jnp.where(merge, amat[c], 0.0)) for c in cs]
        if side_work:
            side_work.pop(0)()
        x_inv = [x_inv[c] - _dot(left[c], block_diag(x_inv[c])) for c in cs]
        blk *= 2
    while side_work:
        side_work.pop(0)()
    sol = [_dot(block_diag(x_inv[c]),
                jnp.concatenate([stack_heads(rhs_v[sls[c]]), stack_heads(rhs_k[sls[c]])], axis=1)) for c in cs]
    w_val = [x[:, :64] for x in sol]
    w_key = [x[:, 64:] for x in sol]
    q_dec = [stack_heads(q_dec_full[sl]) for sl in sls]
    kd_t = [transpose64(stack_heads(k_dec_full[sl])) for sl in sls]
    kd_t_exp = [jnp.where(expand_mask_t, jnp.concatenate([x] * groups, axis=0), 0.0) for x in kd_t]
    g_end_s = [stack_heads(g_end[sl]) for sl in sls]
    s_decay = [jnp.exp(jnp.concatenate(
        [jnp.broadcast_to(x[g * chunk:g * chunk + 1], (64, 64)) for g in range(groups)], axis=0)) for x in g_end_s]

    o_chunks = []
    if nseg > 1:
        s_old = [s_scr[c] for c in cs]
        s_b = [x.astype(BF16) for x in s_old]
        u = [w_val[c] - _dot(expand(w_key[c]), s_b[c]) for c in cs]
        o_chunks = [_dot(expand(q_dec[c]), s_b[c]) + _dot(qk[c], u[c]) for c in cs]
        for c in cs:
            s_scr[c] = s_old[c] * s_decay[c] + _dot_nt(kd_t_exp[c], transpose64(u[c]))
    else:
        state = s_scr[0]
        qk_sol = [_dot(qk[c], sol[c]) for c in cs]
        lhs = [jnp.concatenate([expand(q_dec[c] - qk_sol[c][:, 64:]), _dot(kd_t_exp[c], expand(w_key[c]))], axis=0)
               for c in cs]
        s_add = [_dot(kd_t_exp[c], w_val[c]) for c in cs]
        for c in cs:
            from_state = _dot(lhs[c], state.astype(BF16))
            o_chunks.append(from_state[:STACK] + qk_sol[c][:, :64])
            state = state * s_decay[c] - from_state[STACK:] + s_add[c]
        s_scr[0] = state
    oa = [jnp.concatenate([o[h * 64:(h + 1) * 64] for h in range(A_HEADS)], axis=1) for o in o_chunks]
    oa = oa[0] if nchunks == 1 else jnp.concatenate(oa, axis=0)
    oa = oa * lax.rsqrt(_dot(oa * oa, gavg) + EPS) * onorm_ref[...] * _silu(z[:, Z_GATE:Z_B])

    mix = jnp.concatenate([oa, results["ob"], results["oc"], results["od"]], axis=1)
    x1 = x_in + _rms(_dot(mix, wout_ref[...]), gpost_ref[...])
    x1_ref[...] = x1
    q_ref[...] = _dot(_rms(x1, gpre_ref[...]), wq_ref[...]).astype(q_ref.dtype)

    @pl.when(t == nt - 1)
    def _store_state():
        for s in range(nseg):
            dconv_o[s] = xq_scr[s, qoff - (A_CONV - 1):qoff, :]
            bconv_o[s] = xb_scr[s, boff - (B_CONV - 1):boff, :]
            pbuf_o[s] = xc_scr[s, coff - POOL_BUF:coff, :]
            lconv_o[s] = xd_scr[s, doff - (D_CONV - 1):doff, :]
            lh_o[s] = h_scr[s]
            for h in range(A_HEADS):
                c, sc = divmod(s, seq_per_chunk)
                g = h * seq_per_chunk + sc
                ds_o[s, h] = s_scr[c, g * 64:(g + 1) * 64, :]


def _mixer(x, state_in, layer_in, state_out, layer_out, lw, *, nseg, tn, chunk, pos0):
    bsz, tlen, d = x.shape
    rows = nseg * tn
    assert rows % 64 == 0 and tlen % tn == 0 and bsz % nseg == 0
    assert tn % SUBLANES == 0
    assert (nseg == 1 and chunk == 64) or (nseg > 1 and chunk == tn and tlen == tn)
    groups = STACK // chunk
    grid = (bsz // nseg, tlen // tn)

    def state_spec(a, layer):
        blk = (None, nseg) + a.shape[2:]
        nd = a.ndim
        return pl.BlockSpec(blk, lambda b, t: (layer, b) + (0,) * (nd - 2))

    weights = (lw["norm_mix_pre"], lw["w_in"], lw["conv_qkv"], lw["hp"], lw["onorm"], lw["dw_b"], lw["brow"],
               lw["w_pw"], lw["w_pool"], lw["conv_d"], lw["w_rg"], lw["w_ig"], lw["gavg"],
               lw["w_out"], lw["norm_mix_post"], lw["norm_x_pre"], lw["w_xq"])
    n_in = 1 + len(state_in) + len(weights)
    nt = tlen // tn
    row_spec = pl.BlockSpec((rows, d), lambda b, t: (b * nt + t, 0))
    out_shape = (jax.ShapeDtypeStruct((bsz * tlen, d), F32), jax.ShapeDtypeStruct((bsz * tlen, d), BF16)) + tuple(
        jax.ShapeDtypeStruct(a.shape, a.dtype) for a in state_out)
    kern = functools.partial(_mixer_kernel, nseg=nseg, tn=tn, chunk=chunk, pos0=pos0)
    return pl.pallas_call(
        kern,
        grid=grid,
        in_specs=[pl.BlockSpec((nseg, tn, d), lambda b, t: (b, t, 0))]
        + [state_spec(a, layer_in) for a in state_in] + [_layer_spec(w, layer_out) for w in weights]
        + [pl.BlockSpec(memory_space=pl.ANY) for _ in state_out],
        out_specs=[row_spec, row_spec] + [state_spec(a, layer_out) for a in state_out],
        out_shape=out_shape,
        input_output_aliases={n_in + j: 2 + j for j in range(len(state_out))},
        scratch_shapes=[
            pltpu.VMEM((nseg, 8 + tn + SUBLANES, A_QKV), F32),
            pltpu.VMEM((nseg, 32 + tn + SUBLANES, W_GROUP), F32),
            pltpu.VMEM((nseg, 16 + tn, W_GROUP), F32),
            pltpu.VMEM((nseg, 8 + tn + SUBLANES, W_GROUP), F32),
            pltpu.VMEM((rows // 64 if nseg > 1 else 1, groups * 64, 64), F32),
            pltpu.VMEM((nseg, 1, W_GROUP), F32),
        ],
        compiler_params=_cparams("parallel", "arbitrary"),
        name="mixer",
    )(x, *state_in, *weights, *state_out)


def _softmax_rows(s):
    e = jnp.exp(s - jnp.max(s, axis=-1, keepdims=True))
    return e / jnp.sum(e, axis=-1, keepdims=True)


def _attn_kernel(q_ref, k_ref, v_ref, o_ref):
    q = q_ref[0]
    heads = [slice(h * X_HD, (h + 1) * X_HD) for h in range(X_HEADS)]
    s = [_dot_nt(q[:, sl], k_ref[:, sl]) * (X_HD ** -0.5) for sl in heads]
    p = [_softmax_rows(x) for x in s]
    o = [_dot(p[h], v_ref[:, heads[h]]) for h in range(X_HEADS)]
    o_ref[0] = jnp.concatenate(o, axis=1).astype(o_ref.dtype)


def _attn(q, k_rows, v_rows, layer, tq):
    bsz, tlen, d = q.shape
    tile = lambda b, t: (b, t, 0)
    mem_spec = pl.BlockSpec((None, None, N_MEM, d), lambda b, t: (layer, b, 0, 0))
    return pl.pallas_call(
        _attn_kernel,
        grid=(bsz, tlen // tq),
        in_specs=[pl.BlockSpec((1, tq, d), tile), mem_spec, mem_spec],
        out_specs=pl.BlockSpec((1, tq, d), tile),
        out_shape=jax.ShapeDtypeStruct((bsz, tlen, d), BF16),
        compiler_params=_cparams("parallel", "arbitrary"),
        name="attn",
    )(q, k_rows, v_rows)


def _attn_cache_kernel(q_ref, k_ref, v_ref, o_ref, *, nseq, tlen):
    hrows = X_HEADS * tlen
    qrow_head = lax.broadcasted_iota(jnp.int32, (hrows, N_MEM * X_HEADS), 0) // tlen
    col_head = lax.broadcasted_iota(jnp.int32, (hrows, N_MEM * X_HEADS), 1) % X_HEADS
    own = qrow_head == col_head
    seqs = range(nseq)
    qs = [q_ref[s].astype(F32) for s in seqs]
    q4 = [jnp.concatenate([x[:, h * X_HD:(h + 1) * X_HD] for h in range(X_HEADS)], axis=0) for x in qs]
    sc = [_dot_nt(q4[s], k_ref[0, s].reshape(N_MEM * X_HEADS, X_HD)) * (X_HD ** -0.5) for s in seqs]
    p = [_softmax_rows(jnp.where(own, x, NEG)) for x in sc]
    o4 = [_dot(p[s], v_ref[0, s].reshape(N_MEM * X_HEADS, X_HD)) for s in seqs]
    for s in seqs:
        o_ref[s] = jnp.concatenate([o4[s][h * tlen:(h + 1) * tlen] for h in range(X_HEADS)], axis=1)


def _attn_cache(q, cache_k, cache_v, layer, nseq):
    bsz, tlen, d = q.shape
    cache_blk = (1, nseq) + cache_k.shape[2:]
    cache_map = lambda i: (layer, i, 0, 0, 0)
    kern = functools.partial(_attn_cache_kernel, nseq=nseq, tlen=tlen)
    return pl.pallas_call(
        kern,
        grid=(bsz // nseq,),
        in_specs=[pl.BlockSpec((nseq, tlen, d), lambda i: (i, 0, 0)), pl.BlockSpec(cache_blk, cache_map),
                  pl.BlockSpec(cache_blk, cache_map)],
        out_specs=pl.BlockSpec((nseq, tlen, d), lambda i: (i, 0, 0)),
        out_shape=jax.ShapeDtypeStruct((bsz, tlen, d), F32),
        compiler_params=_cparams("parallel"),
        name="attn_cache",
    )(q, cache_k, cache_v)


def _xo_ffn_kernel(o_ref, x_ref, wo_ref, gxo_ref, gpre_ref, win_ref, wout_ref, gpost_ref, y_ref):
    x = x_ref[...] + _rms(_dot(o_ref[...], wo_ref[...]), gxo_ref[...])
    gu = _dot(_rms(x, gpre_ref[...]), win_ref[...])
    act = _silu(gu[:, :FFN]) * gu[:, FFN:]
    y_ref[...] = x + _rms(_dot(act, wout_ref[...]), gpost_ref[...])


def _xo_ffn(o, x, lw, layer, tm):
    m, d = x.shape
    tm = min(tm, m)
    assert m % tm == 0
    row = lambda i: (i, 0)
    weights = (lw["w_xo"], lw["norm_x_post"], lw["norm_ffn_pre"], lw["w_ffn_in"], lw["w_ffn_out"],
               lw["norm_ffn_post"])
    return pl.pallas_call(
        _xo_ffn_kernel,
        grid=(m // tm,),
        in_specs=[pl.BlockSpec((tm, d), row), pl.BlockSpec((tm, d), row)] + [_layer_spec(w, layer) for w in weights],
        out_specs=pl.BlockSpec((tm, d), row),
        out_shape=jax.ShapeDtypeStruct((m, d), F32),
        compiler_params=_cparams("parallel"),
        name="xo_ffn",
    )(o, x, *weights)


def _block_diag(w):
    depth, n, k, _ = w.shape
    return (w[:, :, :, None, :] * jnp.eye(n, dtype=w.dtype)[None, :, None, :, None]).reshape(depth, n * k, n * k)


def _prep_weights(p):
    w_in = p["w_in"]
    depth = w_in.shape[0]
    off_araw, off_gate = A_QKV, A_QKV + 2 * A_HEADS
    per_head = lambda a: jnp.repeat(a, 64, axis=-1)
    w_in = jnp.concatenate(
        [w_in[..., :off_araw], w_in[..., off_gate:], per_head(w_in[..., off_araw:off_araw + A_HEADS]),
         per_head(w_in[..., off_araw + A_HEADS:off_gate])], axis=-1)
    row = lambda a: a[:, None, :]
    gavg = _block_diag(jnp.full((depth, B_GROUPS, 64, 64), 1.0 / 64, F32))
    return {
        "w_in": w_in.astype(BF16),
        "norm_mix_pre": row(p["norm_mix_pre"]),
        "conv_qkv": p["conv_qkv"],
        "hp": jnp.stack([per_head(p["a_log"]), per_head(p["dt_bias"])], axis=1),
        "onorm": row(jnp.tile(p["onorm_a"], (1, A_HEADS))),
        "dw_b": p["dw_b"],
        "brow": jnp.stack([p["dwbias_b"], p["gn_gain_b"], p["gn_bias_b"], p["scale_pool"],
                           p["conv_bias_d"], p["b_rg"], p["b_ig"], p["lam_d"]], axis=1),
        "w_pw": p["w_pw_b"].astype(BF16),
        "w_pool": _block_diag(p["w_pool"]).astype(BF16),
        "conv_d": p["conv_d"],
        "w_rg": _block_diag(p["w_rg"]).astype(BF16),
        "w_ig": _block_diag(p["w_ig"]).astype(BF16),
        "gavg": gavg.astype(BF16),
        "w_out": p["w_out"].astype(BF16),
        "norm_mix_post": row(p["norm_mix_post"]),
        "norm_x_pre": row(p["norm_x_pre"]),
        "w_xq": p["w_xq"].astype(BF16),
        "w_xo": p["w_xo"].astype(BF16),
        "norm_x_post": row(p["norm_x_post"]),
        "norm_ffn_pre": row(p["norm_ffn_pre"]),
        "w_ffn_in": p["w_ffn_in"].astype(BF16),
        "w_ffn_out": p["w_ffn_out"].astype(BF16),
        "norm_ffn_post": row(p["norm_ffn_post"]),
    }


def _layer(x, attend, state_in, layer_in, state_out, layer, lw, *, nseg, tn, chunk, pos0, tm):
    bsz, tlen, d = x.shape
    x1, q, *new_state = _mixer(x, state_in, layer_in, state_out, layer, lw,
                               nseg=nseg, tn=tn, chunk=chunk, pos0=pos0)
    o = attend(q.reshape(bsz, tlen, d))
    x3 = _xo_ffn(o.reshape(bsz * tlen, d), x1, lw, layer, tm)
    return x3.reshape(bsz, tlen, d), new_state


def kernel(x_prompt, x_sample, mem_prompt, state_delta, state_delta_conv, state_conf_conv, state_pool,
           state_lru_conv, state_lru_h, cache_mem_k, cache_mem_v, norm_mix_pre, norm_mix_post, w_in, conv_qkv,
           a_log, dt_bias, onorm_a, dw_b, dwbias_b, gn_gain_b, gn_bias_b, w_pw_b, w_pool, scale_pool, conv_d,
           conv_bias_d, w_rg, b_rg, w_ig, b_ig, lam_d, w_out, norm_x_pre, norm_x_post, norm_mem, w_xq, w_xkv,
           w_xo, norm_ffn_pre, norm_ffn_post, w_ffn_in, w_ffn_out):
    depth = w_in.shape[0]
    bp, tp, d = x_prompt.shape
    bs, ts, _ = x_sample.shape
    dt = x_prompt.dtype
    per_layer = dict(norm_mix_pre=norm_mix_pre, norm_mix_post=norm_mix_post, w_in=w_in, conv_qkv=conv_qkv,
                     a_log=a_log, dt_bias=dt_bias, onorm_a=onorm_a, dw_b=dw_b, dwbias_b=dwbias_b,
                     gn_gain_b=gn_gain_b, gn_bias_b=gn_bias_b, w_pw_b=w_pw_b, w_pool=w_pool,
                     scale_pool=scale_pool, conv_d=conv_d, conv_bias_d=conv_bias_d, w_rg=w_rg, b_rg=b_rg,
                     w_ig=w_ig, b_ig=b_ig, lam_d=lam_d, w_out=w_out, norm_x_pre=norm_x_pre,
                     norm_x_post=norm_x_post, w_xq=w_xq, w_xo=w_xo, norm_ffn_pre=norm_ffn_pre,
                     norm_ffn_post=norm_ffn_post, w_ffn_in=w_ffn_in, w_ffn_out=w_ffn_out)
    mem_k_p, mem_v_p, k_rows, v_rows = _memory_kv(mem_prompt, norm_mem[:, None, :], w_xkv.astype(BF16))

    def state_shapes(b):
        return ((b, A_CONV - 1, A_QKV), (b, A_HEADS, A_DK, A_DK), (b, B_CONV - 1, W_GROUP), (b, POOL_BUF, W_GROUP),
                (b, D_CONV - 1, W_GROUP), (b, 1, W_GROUP))

    zero_state = tuple(jnp.zeros((1,) + s, dt) for s in state_shapes(bp))
    sample_state = (state_delta_conv, state_delta, state_conf_conv, state_pool, state_lru_conv,
                    state_lru_h[:, :, None, :])
    st_p = tuple(jnp.zeros((depth,) + s, dt) for s in state_shapes(bp))
    st_s = tuple(jnp.zeros((depth,) + s, dt) for s in state_shapes(bs))
    xp, xs = x_prompt, x_sample
    tn_p = min(tp, 512)
    lw = _prep_weights(per_layer)
    for l in range(depth):
        xp, st_p = _layer(xp, lambda q: _attn(q, k_rows, v_rows, l, min(tp, 512)), zero_state, 0, st_p, l, lw,
                          nseg=1, tn=tn_p, chunk=64, pos0=0, tm=512)
        xs, st_s = _layer(xs, lambda q: _attn_cache(q, cache_mem_k, cache_mem_v, l, min(bs, 8)), sample_state, l, st_s, l, lw,
                          nseg=min(bs, 256 // ts), tn=ts, chunk=ts, pos0=PAST_LEN, tm=512)
    dconv_p, delta_p, conf_p, pool_p, lconv_p, lh_p = st_p
    dconv_s, delta_s, conf_s, pool_s, lconv_s, lh_s = st_s
    return (xp, xs, delta_p, delta_s, dconv_p, dconv_s, conf_p, conf_s, pool_p, pool_s,
            lconv_p, lconv_s, lh_p[:, :, 0, :], lh_s[:, :, 0, :], mem_k_p, mem_v_p)
```

```python
import functools

import jax
import jax.numpy as jnp
from jax import lax
from jax.experimental import pallas as pl
from jax.experimental.pallas import tpu as pltpu

F32 = jnp.float32
BF16 = jnp.bfloat16
EPS = 1e-6

D_MODEL = 1024
W_GROUP = 256
A_HEADS = 4
A_DK = 64
A_CONV = 4
A_QKV = 768
B_CONV = 31
B_GROUPS = 4
POOL_BUF = 15
D_CONV = 4
LRU_C = 8.0
N_MEM = 256
X_HEADS = 4
X_HD = 256
FFN = 2816
PAST_LEN = 16384

Z_QKV = 0
Z_GATE = 768
Z_B = 1024
Z_C = 1536
Z_D = 1792
Z_G = 2304
Z_BETA = 2560
Z_COLS = 2816

SUBLANES = 8
STACK = A_HEADS * 64
NEG = -1e30
VMEM_LIMIT = 56 * 1024 * 1024


def _cparams(*sem):
    return pltpu.CompilerParams(dimension_semantics=sem, vmem_limit_bytes=VMEM_LIMIT)


def _layer_spec(a, layer):
    nd = a.ndim
    return pl.BlockSpec((None,) + a.shape[1:], lambda *_: (layer,) + (0,) * (nd - 1),
                        pipeline_mode=pl.Buffered(1))


def _rms(x, g):
    return x * lax.rsqrt(jnp.mean(x * x, axis=-1, keepdims=True) + EPS) * g


def _sigmoid(x):
    return 0.5 * jnp.tanh(0.5 * x) + 0.5


def _silu(x):
    return x * _sigmoid(x)


def _softplus(x):
    return jnp.maximum(x, 0.0) + jnp.log(1.0 + jnp.exp(-jnp.abs(x)))


def _dot(a, b):
    return jnp.dot(a.astype(BF16), b.astype(BF16), preferred_element_type=F32)


def _dot_nt(a, b):
    return lax.dot_general(a.astype(BF16), b.astype(BF16), (((1,), (1,)), ((), ())),
                           preferred_element_type=F32)


def _dot_2pass(x, w):
    hi = x.astype(BF16)
    lo = (x - hi.astype(F32)).astype(BF16)
    return (jnp.dot(hi, w, preferred_element_type=F32) + jnp.dot(lo, w, preferred_element_type=F32))


def _memory_kv_kernel(mem_ref, g_ref, w_ref, k_ref, v_ref, k_rows_ref, v_rows_ref):
    kv = _dot(_rms(mem_ref[0], g_ref[...]), w_ref[...])
    k, v = kv[:, :D_MODEL], kv[:, D_MODEL:]
    k_ref[...] = k.reshape(N_MEM, X_HEADS, X_HD)
    v_ref[...] = v.reshape(N_MEM, X_HEADS, X_HD)
    k_rows_ref[...] = k.astype(k_rows_ref.dtype)
    v_rows_ref[...] = v.astype(v_rows_ref.dtype)


def _memory_kv(mem, g, w):
    bsz, n_mem, d = mem.shape
    depth = w.shape[0]
    cache_spec = pl.BlockSpec((None, None, n_mem, X_HEADS, X_HD), lambda l, b: (l, b, 0, 0, 0))
    rows_spec = pl.BlockSpec((None, None, n_mem, d), lambda l, b: (l, b, 0, 0))
    cache_shape = jax.ShapeDtypeStruct((depth, bsz, n_mem, X_HEADS, X_HD), F32)
    rows_shape = jax.ShapeDtypeStruct((depth, bsz, n_mem, d), BF16)
    return pl.pallas_call(
        _memory_kv_kernel,
        grid=(depth, bsz),
        in_specs=[pl.BlockSpec((1, n_mem, d), lambda l, b: (b, 0, 0)),
                  pl.BlockSpec((None, 1, d), lambda l, b: (l, 0, 0)),
                  pl.BlockSpec((None, d, 2 * d), lambda l, b: (l, 0, 0))],
        out_specs=[cache_spec, cache_spec, rows_spec, rows_spec],
        out_shape=[cache_shape, cache_shape, rows_shape, rows_shape],
        compiler_params=_cparams("parallel", "arbitrary"),
        name="memory_kv",
    )(mem, g, w)


def _seg_scan_sum(x, seg_len, row_in_seg):
    s = 1
    while s < seg_len:
        x = x + jnp.where(row_in_seg >= s, pltpu.roll(x, s, axis=0), 0.0)
        s *= 2
    return x


def _linear_scan(a, b, carries, seg_rows):
    rows, width = a.shape
    nblk = rows // SUBLANES
    a = a.reshape(nblk, SUBLANES, width)
    b = b.reshape(nblk, SUBLANES, width)
    sub = lax.broadcasted_iota(jnp.int32, (1, SUBLANES, 1), 1)
    s = 1
    while s < SUBLANES:
        keep = sub >= s
        a_sh = jnp.where(keep, pltpu.roll(a, s, axis=1), 1.0)
        b_sh = jnp.where(keep, pltpu.roll(b, s, axis=1), 0.0)
        b = a * b_sh + b
        a = a * a_sh
        s *= 2
    blk_per_seg = seg_rows // SUBLANES
    outs = []
    carry = None
    for g in range(nblk):
        if g % blk_per_seg == 0:
            carry = carries[g // blk_per_seg]
        hg = a[g] * carry + b[g]
        carry = hg[SUBLANES - 1:SUBLANES]
        outs.append(hg)
    return jnp.concatenate(outs, axis=0)


def _conv_segment(scr, seg, x, w_rows, width, off, tn, row_block):
    scr[seg, off:off + tn, :] = x
    base = off - (width - 1)
    outs = []
    for r0 in range(0, tn, row_block):
        rb = min(row_block, tn - r0)
        acc = None
        for res in range(SUBLANES):
            part = None
            for p in range(base, base + width):
                if p % SUBLANES != res:
                    continue
                lo = p - res + r0
                term = scr[seg, lo:lo + rb + SUBLANES, :] * w_rows[p - base]
                part = term if part is None else part + term
            if part is None:
                continue
            piece = part[res:res + rb]
            acc = piece if acc is None else acc + piece
        outs.append(acc)
    tail = scr[seg, off + tn - (width - 1):off + tn, :]
    scr[seg, off - (width - 1):off, :] = tail
    return outs[0] if len(outs) == 1 else jnp.concatenate(outs, axis=0)


def _pool_segment(scr, seg, u, off, tn):
    scr[seg, off:off + tn, :] = u
    x = scr[seg, 0:off + tn, :]
    s2 = x + pltpu.roll(x, 1, axis=0)
    s4 = s2 + pltpu.roll(s2, 2, axis=0)
    s8 = s4 + pltpu.roll(s4, 4, axis=0)
    s16 = s8[off:] + s8[off - 8:off - 8 + tn]
    lane = lax.broadcasted_iota(jnp.int32, (1, W_GROUP), 1)
    wsum = jnp.where(lane < 64, s2[off:], jnp.where(lane < 128, s4[off:], jnp.where(lane < 192, s8[off:], s16)))
    tail = scr[seg, off + tn - POOL_BUF:off + tn, :]
    scr[seg, off - POOL_BUF:off, :] = tail
    return wsum


def _mixer_kernel(x_ref, dconv_ref, ds_ref, bconv_ref, pbuf_ref, lconv_ref, lh_ref,
                  gin_ref, win_ref, wqkv_ref, hp_ref, onorm_ref, dw_ref, brow_ref, wpw_ref, wpool_ref, convd_ref,
                  wrg_ref, wig_ref, gavg_ref, wout_ref, gpost_ref, gpre_ref, wq_ref,
                  dconv_all, ds_all, bconv_all, pbuf_all, lconv_all, lh_all,
                  x1_ref, q_ref, dconv_o, ds_o, bconv_o, pbuf_o, lconv_o, lh_o,
                  xq_scr, xb_scr, xc_scr, xd_scr, s_scr, h_scr,
                  *, nseg, tn, chunk, pos0):
    del dconv_all, ds_all, bconv_all, pbuf_all, lconv_all, lh_all
    t = pl.program_id(1)
    nt = pl.num_programs(1)
    rows = nseg * tn
    nchunks = rows // 64
    groups = STACK // chunk
    seq_per_chunk = 64 // chunk
    qoff, boff, coff, doff = 8, 32, 16, 8

    @pl.when(t == 0)
    def _load_state():
        for s in range(nseg):
            xq_scr[s, 0:qoff, :] = jnp.zeros((qoff, A_QKV), F32)
            xq_scr[s, qoff + tn:, :] = jnp.zeros((SUBLANES, A_QKV), F32)
            xb_scr[s, 0:SUBLANES, :] = jnp.zeros((SUBLANES, W_GROUP), F32)
            xb_scr[s, boff + tn:, :] = jnp.zeros((SUBLANES, W_GROUP), F32)
            xc_scr[s, 0:SUBLANES, :] = jnp.zeros((SUBLANES, W_GROUP), F32)
            xd_scr[s, 0:doff, :] = jnp.zeros((doff, W_GROUP), F32)
            xd_scr[s, doff + tn:, :] = jnp.zeros((SUBLANES, W_GROUP), F32)
            xq_scr[s, qoff - (A_CONV - 1):qoff, :] = dconv_ref[s]
            xb_scr[s, boff - (B_CONV - 1):boff, :] = bconv_ref[s]
            xc_scr[s, coff - POOL_BUF:coff, :] = pbuf_ref[s]
            xd_scr[s, doff - (D_CONV - 1):doff, :] = lconv_ref[s]
            h_scr[s] = lh_ref[s]
            for h in range(A_HEADS):
                c, sc = divmod(s, seq_per_chunk)
                g = h * seq_per_chunk + sc
                s_scr[c, g * 64:(g + 1) * 64, :] = ds_ref[s, h]

    x_in = x_ref[0] if nseg == 1 else x_ref[...].reshape(rows, D_MODEL)
    z = _dot(_rms(x_in, gin_ref[...]), win_ref[...])

    def per_segment(fn, x):
        outs = [fn(s, x[s * tn:(s + 1) * tn]) for s in range(nseg)]
        return outs[0] if nseg == 1 else jnp.concatenate(outs, axis=0)

    row = lax.broadcasted_iota(jnp.int32, (rows, 1), 0)
    row_in_seg = row % tn
    brow = brow_ref[...]
    gavg = gavg_ref[...]
    results = {}

    def section_b_conv():
        ub = z[:, Z_B:Z_C]
        glu = ub[:, :W_GROUP] * _sigmoid(ub[:, W_GROUP:])
        dw_rows = [dw_ref[j:j + 1, :] for j in range(B_CONV)]
        yb = per_segment(lambda s, x: _conv_segment(xb_scr, s, x, dw_rows, B_CONV, boff, tn, 64), glu)
        results["yb"] = yb + brow[0:1]

    def section_b_norm():
        yb = results["yb"]
        yc = yb - _dot_2pass(yb, gavg)
        var = _dot(yc * yc, gavg)
        yn = yc * lax.rsqrt(var + EPS) * brow[1:2] + brow[2:3]
        results["ob"] = _dot(_silu(yn), wpw_ref[...])

    def section_c():
        uc = z[:, Z_C:Z_D]
        wsum = per_segment(lambda s, x: _pool_segment(xc_scr, s, x, coff, tn), uc)
        lane = lax.broadcasted_iota(jnp.int32, (1, W_GROUP), 1)
        wlen = jnp.where(lane < 64, 2.0, jnp.where(lane < 128, 4.0, jnp.where(lane < 192, 8.0, 16.0)))
        pos = (pos0 + t * tn + row_in_seg).astype(F32)
        cnt = jnp.minimum(pos + 1.0, wlen)
        results["oc"] = _dot(wsum / cnt - uc, wpool_ref[...]) * brow[3:4]

    def section_d_gates():
        ud = z[:, Z_D:Z_G]
        results["gate_d"] = jax.nn.gelu(ud[:, :W_GROUP])
        cd_rows = [convd_ref[j:j + 1, :] for j in range(D_CONV)]
        xr = per_segment(lambda s, x: _conv_segment(xd_scr, s, x, cd_rows, D_CONV, doff, tn, 64), ud[:, W_GROUP:])
        xr = xr + brow[4:5]
        r_gate = _sigmoid(_dot(xr, wrg_ref[...]) + brow[5:6])
        i_gate = _sigmoid(_dot(xr, wig_ref[...]) + brow[6:7])
        log_a = -LRU_C * r_gate * _softplus(-brow[7:8])
        a = jnp.exp(log_a)
        results["lru_a"] = a
        results["lru_b"] = jnp.sqrt(-jnp.tanh(log_a) * (a * a + 1.0)) * (i_gate * xr)

    def section_d_scan():
        hseq = _linear_scan(results["lru_a"], results["lru_b"], [h_scr[s] for s in range(nseg)], tn)
        results["od"] = results["gate_d"] * hseq
        for s in range(nseg):
            h_scr[s] = hseq[(s + 1) * tn - 1:(s + 1) * tn, :]

    side_work = [section_b_conv, section_b_norm, section_c, section_d_gates, section_d_scan]

    wq_rows = [wqkv_ref[j:j + 1, :] for j in range(A_CONV)]
    yq = per_segment(lambda s, x: _conv_segment(xq_scr, s, x, wq_rows, A_CONV, qoff, tn, 64), z[:, Z_QKV:Z_GATE])
    yq = _silu(yq)
    hp = hp_ref[...]
    g_log = -jnp.exp(hp[0:1]) * _softplus(z[:, Z_G:Z_BETA] + hp[1:2])
    beta = _sigmoid(z[:, Z_BETA:Z_COLS])
    g_cum = _seg_scan_sum(g_log, chunk, row % chunk)
    g_end = jnp.concatenate(
        [jnp.broadcast_to(g_cum[(i + 1) * chunk - 1:(i + 1) * chunk], (chunk, W_GROUP)) for i in range(rows // chunk)],
        axis=0)
    gsum = gavg * 64.0
    qn, kn, vv = yq[:, 0:256], yq[:, 256:512], yq[:, 512:768]
    qn = qn * lax.rsqrt(_dot(qn * qn, gsum) + EPS) * (A_DK ** -0.5)
    kn = kn * lax.rsqrt(_dot(kn * kn, gsum) + EPS)
    e_cum = jnp.exp(g_cum)
    rhs_v = vv * beta
    k_beta = kn * beta
    rhs_k = k_beta * e_cum
    q_dec_full = qn * e_cum
    k_dec_full = kn * jnp.exp(g_end - g_cum)

    ri = lax.broadcasted_iota(jnp.int32, (STACK, STACK), 0)
    ci = lax.broadcasted_iota(jnp.int32, (STACK, STACK), 1)
    same = (ri // chunk) == (ci // chunk)
    causal = same & (ri >= ci)
    strict = same & (ri > ci)
    erow = lax.broadcasted_iota(jnp.int32, (STACK, groups * 64), 0)
    ecol = lax.broadcasted_iota(jnp.int32, (STACK, groups * 64), 1)
    expand_mask = (erow // chunk) == (ecol // 64)
    trow = lax.broadcasted_iota(jnp.int32, (groups * 64, STACK), 0)
    tcol = lax.broadcasted_iota(jnp.int32, (groups * 64, STACK), 1)
    expand_mask_t = (trow // 64) == (tcol // chunk)
    eye64 = (lax.broadcasted_iota(jnp.int32, (64, 64), 0)
             == lax.broadcasted_iota(jnp.int32, (64, 64), 1)).astype(BF16)

    def stack_heads(x):
        return jnp.concatenate([x[:, h * 64:(h + 1) * 64] for h in range(A_HEADS)], axis=0)

    def expand(x):
        return jnp.where(expand_mask, jnp.concatenate([x] * groups, axis=1), 0.0)

    def transpose64(x):
        return lax.dot_general(eye64, x.astype(BF16), (((1,), (1,)), ((), ())), preferred_element_type=F32)

    cs = range(nchunks)
    sls = [slice(c * 64, (c + 1) * 64) for c in cs]
    q = [stack_heads(qn[sl]) for sl in sls]
    k = [stack_heads(kn[sl]) for sl in sls]
    g_s = [stack_heads(g_cum[sl]) for sl in sls]
    g_c = [jnp.concatenate([x] * A_HEADS, axis=1) for x in g_s]
    g_r = [x.T for x in g_c]
    decay = [jnp.exp(jnp.where(causal, g_c[c] - g_r[c], NEG)) for c in cs]
    akq = [_dot_nt(jnp.concatenate([stack_heads(k_beta[sls[c]]), q[c]], axis=0), k[c]) for c in cs]
    amat = [jnp.where(strict, akq[c][:STACK] * decay[c], 0.0) for c in cs]
    qk = [akq[c][STACK:] * decay[c] for c in cs]

    def compress(x):
        return functools.reduce(lambda u, w: u + w, [x[g * chunk:(g + 1) * chunk] for g in range(groups)])

    def block_diag(xc):
        return jnp.where(same, jnp.concatenate([xc] * groups, axis=0), 0.0)

    eye_c = (lax.broadcasted_iota(jnp.int32, (chunk, STACK), 0)
             == lax.broadcasted_iota(jnp.int32, (chunk, STACK), 1) % chunk).astype(F32)
    x_inv = [eye_c - compress(jnp.where((ri // 2) == (ci // 2), amat[c], 0.0)) for c in cs]
    blk = 2
    while blk < chunk:
        merge = ((ri // (2 * blk)) == (ci // (2 * blk))) & ((ri // blk) != (ci // blk))
        left = [_dot(x_inv[c], jnp.where(merge, amat[c], 0.0)) for c in cs]
        if side_work:
            side_work.pop(0)()
        x_inv = [x_inv[c] - _dot(left[c], block_diag(x_inv[c])) for c in cs]
        blk *= 2
    while side_work:
        side_work.pop(0)()
    sol = [_dot(block_diag(x_inv[c]),
                jnp.concatenate([stack_heads(rhs_v[sls[c]]), stack_heads(rhs_k[sls[c]])], axis=1)) for c in cs]
    w_val = [x[:, :64] for x in sol]
    w_key = [x[:, 64:] for x in sol]
    q_dec = [stack_heads(q_dec_full[sl]) for sl in sls]
    kd_t = [transpose64(stack_heads(k_dec_full[sl])) for sl in sls]
    kd_t_exp = [jnp.where(expand_mask_t, jnp.concatenate([x] * groups, axis=0), 0.0) for x in kd_t]
    g_end_s = [stack_heads(g_end[sl]) for sl in sls]
    s_decay = [jnp.exp(jnp.concatenate(
        [jnp.broadcast_to(x[g * chunk:g * chunk + 1], (64, 64)) for g in range(groups)], axis=0)) for x in g_end_s]

    o_chunks = []
    if nseg > 1:
        s_old = [s_scr[c] for c in cs]
        s_b = [x.astype(BF16) for x in s_old]
        u = [w_val[c] - _dot(expand(w_key[c]), s_b[c]) for c in cs]
        o_chunks = [_dot(expand(q_dec[c]), s_b[c]) + _dot(qk[c], u[c]) for c in cs]
        for c in cs:
            s_scr[c] = s_old[c] * s_decay[c] + _dot_nt(kd_t_exp[c], transpose64(u[c]))
    else:
        state = s_scr[0]
        qk_sol = [_dot(qk[c], sol[c]) for c in cs]
        lhs = [jnp.concatenate([expand(q_dec[c] - qk_sol[c][:, 64:]), _dot(kd_t_exp[c], expand(w_key[c]))], axis=0)
               for c in cs]
        s_add = [_dot(kd_t_exp[c], w_val[c]) for c in cs]
        for c in cs:
            from_state = _dot(lhs[c], state.astype(BF16))
            o_chunks.append(from_state[:STACK] + qk_sol[c][:, :64])
            state = state * s_decay[c] - from_state[STACK:] + s_add[c]
        s_scr[0] = state
    oa = [jnp.concatenate([o[h * 64:(h + 1) * 64] for h in range(A_HEADS)], axis=1) for o in o_chunks]
    oa = oa[0] if nchunks == 1 else jnp.concatenate(oa, axis=0)
    oa = oa * lax.rsqrt(_dot(oa * oa, gavg) + EPS) * onorm_ref[...] * _silu(z[:, Z_GATE:Z_B])

    mix = jnp.concatenate([oa, results["ob"], results["oc"], results["od"]], axis=1)
    x1 = x_in + _rms(_dot(mix, wout_ref[...]), gpost_ref[...])
    x1_ref[...] = x1
    q_ref[...] = _dot(_rms(x1, gpre_ref[...]), wq_ref[...]).astype(q_ref.dtype)

    @pl.when(t == nt - 1)
    def _store_state():
        for s in range(nseg):
            dconv_o[s] = xq_scr[s, qoff - (A_CONV - 1):qoff, :]
            bconv_o[s] = xb_scr[s, boff - (B_CONV - 1):boff, :]
            pbuf_o[s] = xc_scr[s, coff - POOL_BUF:coff, :]
            lconv_o[s] = xd_scr[s, doff - (D_CONV - 1):doff, :]
            lh_o[s] = h_scr[s]
            for h in range(A_HEADS):
                c, sc = divmod(s, seq_per_chunk)
                g = h * seq_per_chunk + sc
                ds_o[s, h] = s_scr[c, g * 64:(g + 1) * 64, :]


def _mixer(x, state_in, layer_in, state_out, layer_out, lw, *, nseg, tn, chunk, pos0):
    bsz, tlen, d = x.shape
    rows = nseg * tn
    assert rows % 64 == 0 and tlen % tn == 0 and bsz % nseg == 0
    assert tn % SUBLANES == 0
    assert (nseg == 1 and chunk == 64) or (nseg > 1 and chunk == tn and tlen == tn)
    groups = STACK // chunk
    grid = (bsz // nseg, tlen // tn)

    def state_spec(a, layer):
        blk = (None, nseg) + a.shape[2:]
        nd = a.ndim
        return pl.BlockSpec(blk, lambda b, t: (layer, b) + (0,) * (nd - 2))

    weights = (lw["norm_mix_pre"], lw["w_in"], lw["conv_qkv"], lw["hp"], lw["onorm"], lw["dw_b"], lw["brow"],
               lw["w_pw"], lw["w_pool"], lw["conv_d"], lw["w_rg"], lw["w_ig"], lw["gavg"],
               lw["w_out"], lw["norm_mix_post"], lw["norm_x_pre"], lw["w_xq"])
    n_in = 1 + len(state_in) + len(weights)
    nt = tlen // tn
    row_spec = pl.BlockSpec((rows, d), lambda b, t: (b * nt + t, 0))
    out_shape = (jax.ShapeDtypeStruct((bsz * tlen, d), F32), jax.ShapeDtypeStruct((bsz * tlen, d), BF16)) + tuple(
        jax.ShapeDtypeStruct(a.shape, a.dtype) for a in state_out)
    kern = functools.partial(_mixer_kernel, nseg=nseg, tn=tn, chunk=chunk, pos0=pos0)
    return pl.pallas_call(
        kern,
        grid=grid,
        in_specs=[pl.BlockSpec((nseg, tn, d), lambda b, t: (b, t, 0))]
        + [state_spec(a, layer_in) for a in state_in] + [_layer_spec(w, layer_out) for w in weights]
        + [pl.BlockSpec(memory_space=pl.ANY) for _ in state_out],
        out_specs=[row_spec, row_spec] + [state_spec(a, layer_out) for a in state_out],
        out_shape=out_shape,
        input_output_aliases={n_in + j: 2 + j for j in range(len(state_out))},
        scratch_shapes=[
            pltpu.VMEM((nseg, 8 + tn + SUBLANES, A_QKV), F32),
            pltpu.VMEM((nseg, 32 + tn + SUBLANES, W_GROUP), F32),
            pltpu.VMEM((nseg, 16 + tn, W_GROUP), F32),
            pltpu.VMEM((nseg, 8 + tn + SUBLANES, W_GROUP), F32),
            pltpu.VMEM((rows // 64 if nseg > 1 else 1, groups * 64, 64), F32),
            pltpu.VMEM((nseg, 1, W_GROUP), F32),
        ],
        compiler_params=_cparams("parallel", "arbitrary"),
        name="mixer",
    )(x, *state_in, *weights, *state_out)


def _softmax_rows(s):
    e = jnp.exp(s - jnp.max(s, axis=-1, keepdims=True))
    return e / jnp.sum(e, axis=-1, keepdims=True)


def _attn_kernel(q_ref, k_ref, v_ref, o_ref):
    q = q_ref[0]
    heads = [slice(h * X_HD, (h + 1) * X_HD) for h in range(X_HEADS)]
    s = [_dot_nt(q[:, sl], k_ref[:, sl]) * (X_HD ** -0.5) for sl in heads]
    p = [_softmax_rows(x) for x in s]
    o = [_dot(p[h], v_ref[:, heads[h]]) for h in range(X_HEADS)]
    o_ref[0] = jnp.concatenate(o, axis=1).astype(o_ref.dtype)


def _attn(q, k_rows, v_rows, layer, tq):
    bsz, tlen, d = q.shape
    tile = lambda b, t: (b, t, 0)
    mem_spec = pl.BlockSpec((None, None, N_MEM, d), lambda b, t: (layer, b, 0, 0))
    return pl.pallas_call(
        _attn_kernel,
        grid=(bsz, tlen // tq),
        in_specs=[pl.BlockSpec((1, tq, d), tile), mem_spec, mem_spec],
        out_specs=pl.BlockSpec((1, tq, d), tile),
        out_shape=jax.ShapeDtypeStruct((bsz, tlen, d), BF16),
        compiler_params=_cparams("parallel", "arbitrary"),
        name="attn",
    )(q, k_rows, v_rows)


def _attn_cache_kernel(q_ref, k_ref, v_ref, o_ref, *, nseq, tlen):
    hrows = X_HEADS * tlen
    qrow_head = lax.broadcasted_iota(jnp.int32, (hrows, N_MEM * X_HEADS), 0) // tlen
    col_head = lax.broadcasted_iota(jnp.int32, (hrows, N_MEM * X_HEADS), 1) % X_HEADS
    own = qrow_head == col_head
    seqs = range(nseq)
    qs = [q_ref[s].astype(F32) for s in seqs]
    q4 = [jnp.concatenate([x[:, h * X_HD:(h + 1) * X_HD] for h in range(X_HEADS)], axis=0) for x in qs]
    sc = [_dot_nt(q4[s], k_ref[0, s].reshape(N_MEM * X_HEADS, X_HD)) * (X_HD ** -0.5) for s in seqs]
    p = [_softmax_rows(jnp.where(own, x, NEG)) for x in sc]
    o4 = [_dot(p[s], v_ref[0, s].reshape(N_MEM * X_HEADS, X_HD)) for s in seqs]
    for s in seqs:
        o_ref[s] = jnp.concatenate([o4[s][h * tlen:(h + 1) * tlen] for h in range(X_HEADS)], axis=1)


def _xo_ffn_kernel(o_ref, x_ref, wo_ref, gxo_ref, gpre_ref, win_ref, wout_ref, gpost_ref, y_ref):
    x = x_ref[...] + _rms(_dot(o_ref[...], wo_ref[...]), gxo_ref[...])
    gu = _dot(_rms(x, gpre_ref[...]), win_ref[...])
    act = _silu(gu[:, :FFN]) * gu[:, FFN:]
    y_ref[...] = x + _rms(_dot(act, wout_ref[...]), gpost_ref[...])


def _xo_ffn_with_cache_attn_kernel(o_ref, x_ref, wo_ref, gxo_ref, gpre_ref, win_ref, wout_ref, gpost_ref,
                                   q_ref, k_ref, v_ref, y_ref, oc_ref, *, nseq, tlen):
    _attn_cache_kernel(q_ref, k_ref, v_ref, oc_ref, nseq=nseq, tlen=tlen)
    _xo_ffn_kernel(o_ref, x_ref, wo_ref, gxo_ref, gpre_ref, win_ref, wout_ref, gpost_ref, y_ref)


def _xo_ffn(o, x, lw, layer, tm, cache_attn=None):
    m, d = x.shape
    tm = min(tm, m)
    assert m % tm == 0
    steps = m // tm
    row = lambda i: (i, 0)
    weights = (lw["w_xo"], lw["norm_x_post"], lw["norm_ffn_pre"], lw["w_ffn_in"], lw["w_ffn_out"],
               lw["norm_ffn_post"])
    in_specs = [pl.BlockSpec((tm, d), row), pl.BlockSpec((tm, d), row)] + [_layer_spec(w, layer) for w in weights]
    if cache_attn is None:
        return pl.pallas_call(
            _xo_ffn_kernel,
            grid=(steps,),
            in_specs=in_specs,
            out_specs=pl.BlockSpec((tm, d), row),
            out_shape=jax.ShapeDtypeStruct((m, d), F32),
            compiler_params=_cparams("parallel"),
            name="xo_ffn",
        )(o, x, *weights)
    q, cache_k, cache_v = cache_attn
    bsz, tlen, _ = q.shape
    assert bsz % steps == 0
    nseq = bsz // steps
    cache_spec = pl.BlockSpec((1, nseq) + cache_k.shape[2:], lambda i: (layer, i, 0, 0, 0))
    seq_spec = pl.BlockSpec((nseq, tlen, d), lambda i: (i, 0, 0))
    kern = functools.partial(_xo_ffn_with_cache_attn_kernel, nseq=nseq, tlen=tlen)
    return pl.pallas_call(
        kern,
        grid=(steps,),
        in_specs=in_specs + [seq_spec, cache_spec, cache_spec],
        out_specs=[pl.BlockSpec((tm, d), row), seq_spec],
        out_shape=[jax.ShapeDtypeStruct((m, d), F32), jax.ShapeDtypeStruct((bsz, tlen, d), F32)],
        compiler_params=_cparams("parallel"),
        name="xo_ffn_cache_attn",
    )(o, x, *weights, q, cache_k, cache_v)


def _block_diag(w):
    depth, n, k, _ = w.shape
    return (w[:, :, :, None, :] * jnp.eye(n, dtype=w.dtype)[None, :, None, :, None]).reshape(depth, n * k, n * k)


def _prep_weights(p):
    w_in = p["w_in"]
    depth = w_in.shape[0]
    off_araw, off_gate = A_QKV, A_QKV + 2 * A_HEADS
    per_head = lambda a: jnp.repeat(a, 64, axis=-1)
    w_in = jnp.concatenate(
        [w_in[..., :off_araw], w_in[..., off_gate:], per_head(w_in[..., off_araw:off_araw + A_HEADS]),
         per_head(w_in[..., off_araw + A_HEADS:off_gate])], axis=-1)
    row = lambda a: a[:, None, :]
    gavg = _block_diag(jnp.full((depth, B_GROUPS, 64, 64), 1.0 / 64, F32))
    return {
        "w_in": w_in.astype(BF16),
        "norm_mix_pre": row(p["norm_mix_pre"]),
        "conv_qkv": p["conv_qkv"],
        "hp": jnp.stack([per_head(p["a_log"]), per_head(p["dt_bias"])], axis=1),
        "onorm": row(jnp.tile(p["onorm_a"], (1, A_HEADS))),
        "dw_b": p["dw_b"],
        "brow": jnp.stack([p["dwbias_b"], p["gn_gain_b"], p["gn_bias_b"], p["scale_pool"],
                           p["conv_bias_d"], p["b_rg"], p["b_ig"], p["lam_d"]], axis=1),
        "w_pw": p["w_pw_b"].astype(BF16),
        "w_pool": _block_diag(p["w_pool"]).astype(BF16),
        "conv_d": p["conv_d"],
        "w_rg": _block_diag(p["w_rg"]).astype(BF16),
        "w_ig": _block_diag(p["w_ig"]).astype(BF16),
        "gavg": gavg.astype(BF16),
        "w_out": p["w_out"].astype(BF16),
        "norm_mix_post": row(p["norm_mix_post"]),
        "norm_x_pre": row(p["norm_x_pre"]),
        "w_xq": p["w_xq"].astype(BF16),
        "w_xo": p["w_xo"].astype(BF16),
        "norm_x_post": row(p["norm_x_post"]),
        "norm_ffn_pre": row(p["norm_ffn_pre"]),
        "w_ffn_in": p["w_ffn_in"].astype(BF16),
        "w_ffn_out": p["w_ffn_out"].astype(BF16),
        "norm_ffn_post": row(p["norm_ffn_post"]),
    }


def kernel(x_prompt, x_sample, mem_prompt, state_delta, state_delta_conv, state_conf_conv, state_pool,
           state_lru_conv, state_lru_h, cache_mem_k, cache_mem_v, norm_mix_pre, norm_mix_post, w_in, conv_qkv,
           a_log, dt_bias, onorm_a, dw_b, dwbias_b, gn_gain_b, gn_bias_b, w_pw_b, w_pool, scale_pool, conv_d,
           conv_bias_d, w_rg, b_rg, w_ig, b_ig, lam_d, w_out, norm_x_pre, norm_x_post, norm_mem, w_xq, w_xkv,
           w_xo, norm_ffn_pre, norm_ffn_post, w_ffn_in, w_ffn_out):
    depth = w_in.shape[0]
    bp, tp, d = x_prompt.shape
    bs, ts, _ = x_sample.shape
    dt = x_prompt.dtype
    per_layer = dict(norm_mix_pre=norm_mix_pre, norm_mix_post=norm_mix_post, w_in=w_in, conv_qkv=conv_qkv,
                     a_log=a_log, dt_bias=dt_bias, onorm_a=onorm_a, dw_b=dw_b, dwbias_b=dwbias_b,
                     gn_gain_b=gn_gain_b, gn_bias_b=gn_bias_b, w_pw_b=w_pw_b, w_pool=w_pool,
                     scale_pool=scale_pool, conv_d=conv_d, conv_bias_d=conv_bias_d, w_rg=w_rg, b_rg=b_rg,
                     w_ig=w_ig, b_ig=b_ig, lam_d=lam_d, w_out=w_out, norm_x_pre=norm_x_pre,
                     norm_x_post=norm_x_post, w_xq=w_xq, w_xo=w_xo, norm_ffn_pre=norm_ffn_pre,
                     norm_ffn_post=norm_ffn_post, w_ffn_in=w_ffn_in, w_ffn_out=w_ffn_out)
    mem_k_p, mem_v_p, k_rows, v_rows = _memory_kv(mem_prompt, norm_mem[:, None, :], w_xkv.astype(BF16))

    def state_shapes(b):
        return ((b, A_CONV - 1, A_QKV), (b, A_HEADS, A_DK, A_DK), (b, B_CONV - 1, W_GROUP), (b, POOL_BUF, W_GROUP),
                (b, D_CONV - 1, W_GROUP), (b, 1, W_GROUP))

    zero_state = tuple(jnp.zeros((1,) + s, dt) for s in state_shapes(bp))
    sample_state = (state_delta_conv, state_delta, state_conf_conv, state_pool, state_lru_conv,
                    state_lru_h[:, :, None, :])
    st_p = tuple(jnp.zeros((depth,) + s, dt) for s in state_shapes(bp))
    st_s = tuple(jnp.zeros((depth,) + s, dt) for s in state_shapes(bs))
    xp, xs = x_prompt, x_sample
    tn_p = min(tp, 512)
    lw = _prep_weights(per_layer)
    nseg_s = min(bs, 256 // ts)
    for l in range(depth):
        x1_p, q_p, *st_p = _mixer(xp, zero_state, 0, st_p, l, lw, nseg=1, tn=tn_p, chunk=64, pos0=0)
        x1_s, q_s, *st_s = _mixer(xs, sample_state, l, st_s, l, lw, nseg=nseg_s, tn=ts, chunk=ts, pos0=PAST_LEN)
        o_p = _attn(q_p.reshape(bp, tp, d), k_rows, v_rows, l, min(tp, 512))
        xp, o_s = _xo_ffn(o_p.reshape(bp * tp, d), x1_p, lw, l, 512,
                          cache_attn=(q_s.reshape(bs, ts, d), cache_mem_k, cache_mem_v))
        xs = _xo_ffn(o_s.reshape(bs * ts, d), x1_s, lw, l, 512).reshape(bs, ts, d)
        xp = xp.reshape(bp, tp, d)
    dconv_p, delta_p, conf_p, pool_p, lconv_p, lh_p = st_p
    dconv_s, delta_s, conf_s, pool_s, lconv_s, lh_s = st_s
    return (xp, xs, delta_p, delta_s, dconv_p, dconv_s, conf_p, conf_s, pool_p, pool_s,
            lconv_p, lconv_s, lh_p[:, :, 0, :], lh_s[:, :, 0, :], mem_k_p, mem_v_p)
```

```python
import functools

import jax
import jax.numpy as jnp
from jax import lax
from jax.experimental import pallas as pl
from jax.experimental.pallas import tpu as pltpu

F32 = jnp.float32
BF16 = jnp.bfloat16
EPS = 1e-6

D_MODEL = 1024
W_GROUP = 256
A_HEADS = 4
A_DK = 64
A_CONV = 4
A_QKV = 768
B_CONV = 31
B_GROUPS = 4
POOL_BUF = 15
D_CONV = 4
LRU_C = 8.0
N_MEM = 256
X_HEADS = 4
X_HD = 256
FFN = 2816
PAST_LEN = 16384

Z_QKV = 0
Z_GATE = 768
Z_B = 1024
Z_C = 1536
Z_D = 1792
Z_G = 2304
Z_BETA = 2560
Z_COLS = 2816

SUBLANES = 8
STACK = A_HEADS * 64
NEG = -1e30
VMEM_LIMIT = 56 * 1024 * 1024


def _cparams(*sem):
    return pltpu.CompilerParams(dimension_semantics=sem, vmem_limit_bytes=VMEM_LIMIT)


def _layer_spec(a, layer):
    nd = a.ndim
    return pl.BlockSpec((None,) + a.shape[1:], lambda *_: (layer,) + (0,) * (nd - 1),
                        pipeline_mode=pl.Buffered(1))


def _rms(x, g):
    return x * lax.rsqrt(jnp.mean(x * x, axis=-1, keepdims=True) + EPS) * g


def _sigmoid(x):
    return 0.5 * jnp.tanh(0.5 * x) + 0.5


def _silu(x):
    return x * _sigmoid(x)


def _softplus(x):
    return jnp.maximum(x, 0.0) + jnp.log(1.0 + jnp.exp(-jnp.abs(x)))


def _dot(a, b):
    return jnp.dot(a.astype(BF16), b.astype(BF16), preferred_element_type=F32)


def _dot_nt(a, b):
    return lax.dot_general(a.astype(BF16), b.astype(BF16), (((1,), (1,)), ((), ())),
                           preferred_element_type=F32)


def _dot_2pass(x, w):
    hi = x.astype(BF16)
    lo = (x - hi.astype(F32)).astype(BF16)
    return (jnp.dot(hi, w, preferred_element_type=F32) + jnp.dot(lo, w, preferred_element_type=F32))


def _memory_kv_kernel(mem_ref, g_ref, w_ref, k_ref, v_ref, k_rows_ref, v_rows_ref):
    kv = _dot(_rms(mem_ref[0], g_ref[...]), w_ref[...])
    k, v = kv[:, :D_MODEL], kv[:, D_MODEL:]
    k_ref[...] = k.reshape(N_MEM, X_HEADS, X_HD)
    v_ref[...] = v.reshape(N_MEM, X_HEADS, X_HD)
    k_rows_ref[...] = k.astype(k_rows_ref.dtype)
    v_rows_ref[...] = v.astype(v_rows_ref.dtype)


def _memory_kv(mem, g, w):
    bsz, n_mem, d = mem.shape
    depth = w.shape[0]
    cache_spec = pl.BlockSpec((None, None, n_mem, X_HEADS, X_HD), lambda l, b: (l, b, 0, 0, 0))
    rows_spec = pl.BlockSpec((None, None, n_mem, d), lambda l, b: (l, b, 0, 0))
    cache_shape = jax.ShapeDtypeStruct((depth, bsz, n_mem, X_HEADS, X_HD), F32)
    rows_shape = jax.ShapeDtypeStruct((depth, bsz, n_mem, d), BF16)
    return pl.pallas_call(
        _memory_kv_kernel,
        grid=(depth, bsz),
        in_specs=[pl.BlockSpec((1, n_mem, d), lambda l, b: (b, 0, 0)),
                  pl.BlockSpec((None, 1, d), lambda l, b: (l, 0, 0)),
                  pl.BlockSpec((None, d, 2 * d), lambda l, b: (l, 0, 0))],
        out_specs=[cache_spec, cache_spec, rows_spec, rows_spec],
        out_shape=[cache_shape, cache_shape, rows_shape, rows_shape],
        compiler_params=_cparams("parallel", "arbitrary"),
        name="memory_kv",
    )(mem, g, w)


def _seg_scan_sum(x, seg_len, row_in_seg):
    s = 1
    while s < seg_len:
        x = x + jnp.where(row_in_seg >= s, pltpu.roll(x, s, axis=0), 0.0)
        s *= 2
    return x


def _linear_scan(a, b, carries, seg_rows):
    rows, width = a.shape
    nblk = rows // SUBLANES
    a = a.reshape(nblk, SUBLANES, width)
    b = b.reshape(nblk, SUBLANES, width)
    sub = lax.broadcasted_iota(jnp.int32, (1, SUBLANES, 1), 1)
    s = 1
    while s < SUBLANES:
        keep = sub >= s
        a_sh = jnp.where(keep, pltpu.roll(a, s, axis=1), 1.0)
        b_sh = jnp.where(keep, pltpu.roll(b, s, axis=1), 0.0)
        b = a * b_sh + b
        a = a * a_sh
        s *= 2
    blk_per_seg = seg_rows // SUBLANES
    outs = []
    carry = None
    for g in range(nblk):
        if g % blk_per_seg == 0:
            carry = carries[g // blk_per_seg]
        hg = a[g] * carry + b[g]
        carry = hg[SUBLANES - 1:SUBLANES]
        outs.append(hg)
    return jnp.concatenate(outs, axis=0)


def _conv_segment(scr, seg, x, w_rows, width, off, tn, row_block):
    scr[seg, off:off + tn, :] = x
    base = off - (width - 1)
    outs = []
    for r0 in range(0, tn, row_block):
        rb = min(row_block, tn - r0)
        acc = None
        for res in range(SUBLANES):
            part = None
            for p in range(base, base + width):
                if p % SUBLANES != res:
                    continue
                lo = p - res + r0
                term = scr[seg, lo:lo + rb + SUBLANES, :] * w_rows[p - base]
                part = term if part is None else part + term
            if part is None:
                continue
            piece = part[res:res + rb]
            acc = piece if acc is None else acc + piece
        outs.append(acc)
    tail = scr[seg, off + tn - (width - 1):off + tn, :]
    scr[seg, off - (width - 1):off, :] = tail
    return outs[0] if len(outs) == 1 else jnp.concatenate(outs, axis=0)


def _pool_segment(scr, seg, u, off, tn):
    scr[seg, off:off + tn, :] = u
    x = scr[seg, 0:off + tn, :]
    s2 = x + pltpu.roll(x, 1, axis=0)
    s4 = s2 + pltpu.roll(s2, 2, axis=0)
    s8 = s4 + pltpu.roll(s4, 4, axis=0)
    s16 = s8[off:] + s8[off - 8:off - 8 + tn]
    lane = lax.broadcasted_iota(jnp.int32, (1, W_GROUP), 1)
    wsum = jnp.where(lane < 64, s2[off:], jnp.where(lane < 128, s4[off:], jnp.where(lane < 192, s8[off:], s16)))
    tail = scr[seg, off + tn - POOL_BUF:off + tn, :]
    scr[seg, off - POOL_BUF:off, :] = tail
    return wsum


def _softmax_rows(s):
    e = jnp.exp(s - jnp.max(s, axis=-1, keepdims=True))
    return e / jnp.sum(e, axis=-1, keepdims=True)


def _attend(q, k_ref, v_ref):
    heads = [slice(h * X_HD, (h + 1) * X_HD) for h in range(X_HEADS)]
    s = [_dot_nt(q[:, sl], k_ref[:, sl]) * (X_HD ** -0.5) for sl in heads]
    p = [_softmax_rows(x) for x in s]
    return jnp.concatenate([_dot(p[h], v_ref[:, heads[h]]) for h in range(X_HEADS)], axis=1)


def _mixer_kernel(x_ref, dconv_ref, ds_ref, bconv_ref, pbuf_ref, lconv_ref, lh_ref,
                  gin_ref, win_ref, wqkv_ref, hp_ref, onorm_ref, dw_ref, brow_ref, wpw_ref, wpool_ref, convd_ref,
                  wrg_ref, wig_ref, gavg_ref, wout_ref, gpost_ref, gpre_ref, wq_ref,
                  *rest, nseg, tn, chunk, pos0, attend):
    if attend:
        kmem_ref, vmem_ref, *rest = rest
    (dconv_all, ds_all, bconv_all, pbuf_all, lconv_all, lh_all,
     x1_ref, q_ref, dconv_o, ds_o, bconv_o, pbuf_o, lconv_o, lh_o,
     xq_scr, xb_scr, xc_scr, xd_scr, s_scr, h_scr) = rest
    del dconv_all, ds_all, bconv_all, pbuf_all, lconv_all, lh_all
    t = pl.program_id(1)
    nt = pl.num_programs(1)
    rows = nseg * tn
    nchunks = rows // 64
    groups = STACK // chunk
    seq_per_chunk = 64 // chunk
    qoff, boff, coff, doff = 8, 32, 16, 8

    @pl.when(t == 0)
    def _load_state():
        for s in range(nseg):
            xq_scr[s, 0:qoff, :] = jnp.zeros((qoff, A_QKV), F32)
            xq_scr[s, qoff + tn:, :] = jnp.zeros((SUBLANES, A_QKV), F32)
            xb_scr[s, 0:SUBLANES, :] = jnp.zeros((SUBLANES, W_GROUP), F32)
            xb_scr[s, boff + tn:, :] = jnp.zeros((SUBLANES, W_GROUP), F32)
            xc_scr[s, 0:SUBLANES, :] = jnp.zeros((SUBLANES, W_GROUP), F32)
            xd_scr[s, 0:doff, :] = jnp.zeros((doff, W_GROUP), F32)
            xd_scr[s, doff + tn:, :] = jnp.zeros((SUBLANES, W_GROUP), F32)
            xq_scr[s, qoff - (A_CONV - 1):qoff, :] = dconv_ref[s]
            xb_scr[s, boff - (B_CONV - 1):boff, :] = bconv_ref[s]
            xc_scr[s, coff - POOL_BUF:coff, :] = pbuf_ref[s]
            xd_scr[s, doff - (D_CONV - 1):doff, :] = lconv_ref[s]
            h_scr[s] = lh_ref[s]
            for h in range(A_HEADS):
                c, sc = divmod(s, seq_per_chunk)
                g = h * seq_per_chunk + sc
                s_scr[c, g * 64:(g + 1) * 64, :] = ds_ref[s, h]

    x_in = x_ref[0] if nseg == 1 else x_ref[...].reshape(rows, D_MODEL)
    z = _dot(_rms(x_in, gin_ref[...]), win_ref[...])

    def per_segment(fn, x):
        outs = [fn(s, x[s * tn:(s + 1) * tn]) for s in range(nseg)]
        return outs[0] if nseg == 1 else jnp.concatenate(outs, axis=0)

    row = lax.broadcasted_iota(jnp.int32, (rows, 1), 0)
    row_in_seg = row % tn
    brow = brow_ref[...]
    gavg = gavg_ref[...]
    results = {}

    def section_b_conv():
        ub = z[:, Z_B:Z_C]
        glu = ub[:, :W_GROUP] * _sigmoid(ub[:, W_GROUP:])
        dw_rows = [dw_ref[j:j + 1, :] for j in range(B_CONV)]
        yb = per_segment(lambda s, x: _conv_segment(xb_scr, s, x, dw_rows, B_CONV, boff, tn, 64), glu)
        results["yb"] = yb + brow[0:1]

    def section_b_norm():
        yb = results["yb"]
        yc = yb - _dot_2pass(yb, gavg)
        var = _dot(yc * yc, gavg)
        yn = yc * lax.rsqrt(var + EPS) * brow[1:2] + brow[2:3]
        results["ob"] = _dot(_silu(yn), wpw_ref[...])

    def section_c():
        uc = z[:, Z_C:Z_D]
        wsum = per_segment(lambda s, x: _pool_segment(xc_scr, s, x, coff, tn), uc)
        lane = lax.broadcasted_iota(jnp.int32, (1, W_GROUP), 1)
        wlen = jnp.where(lane < 64, 2.0, jnp.where(lane < 128, 4.0, jnp.where(lane < 192, 8.0, 16.0)))
        pos = (pos0 + t * tn + row_in_seg).astype(F32)
        cnt = jnp.minimum(pos + 1.0, wlen)
        results["oc"] = _dot(wsum / cnt - uc, wpool_ref[...]) * brow[3:4]

    def section_d_gates():
        ud = z[:, Z_D:Z_G]
        results["gate_d"] = jax.nn.gelu(ud[:, :W_GROUP])
        cd_rows = [convd_ref[j:j + 1, :] for j in range(D_CONV)]
        xr = per_segment(lambda s, x: _conv_segment(xd_scr, s, x, cd_rows, D_CONV, doff, tn, 64), ud[:, W_GROUP:])
        xr = xr + brow[4:5]
        r_gate = _sigmoid(_dot(xr, wrg_ref[...]) + brow[5:6])
        i_gate = _sigmoid(_dot(xr, wig_ref[...]) + brow[6:7])
        log_a = -LRU_C * r_gate * _softplus(-brow[7:8])
        a = jnp.exp(log_a)
        results["lru_a"] = a
        results["lru_b"] = jnp.sqrt(-jnp.tanh(log_a) * (a * a + 1.0)) * (i_gate * xr)

    def section_d_scan():
        hseq = _linear_scan(results["lru_a"], results["lru_b"], [h_scr[s] for s in range(nseg)], tn)
        results["od"] = results["gate_d"] * hseq
        for s in range(nseg):
            h_scr[s] = hseq[(s + 1) * tn - 1:(s + 1) * tn, :]

    side_work = [section_b_conv, section_b_norm, section_c, section_d_gates, section_d_scan]

    wq_rows = [wqkv_ref[j:j + 1, :] for j in range(A_CONV)]
    yq = per_segment(lambda s, x: _conv_segment(xq_scr, s, x, wq_rows, A_CONV, qoff, tn, 64), z[:, Z_QKV:Z_GATE])
    yq = _silu(yq)
    hp = hp_ref[...]
    g_log = -jnp.exp(hp[0:1]) * _softplus(z[:, Z_G:Z_BETA] + hp[1:2])
    beta = _sigmoid(z[:, Z_BETA:Z_COLS])
    g_cum = _seg_scan_sum(g_log, chunk, row % chunk)
    g_end = jnp.concatenate(
        [jnp.broadcast_to(g_cum[(i + 1) * chunk - 1:(i + 1) * chunk], (chunk, W_GROUP)) for i in range(rows // chunk)],
        axis=0)
    gsum = gavg * 64.0
    qn, kn, vv = yq[:, 0:256], yq[:, 256:512], yq[:, 512:768]
    qn = qn * lax.rsqrt(_dot(qn * qn, gsum) + EPS) * (A_DK ** -0.5)
    kn = kn * lax.rsqrt(_dot(kn * kn, gsum) + EPS)
    e_cum = jnp.exp(g_cum)
    rhs_v = vv * beta
    k_beta = kn * beta
    rhs_k = k_beta * e_cum
    q_dec_full = qn * e_cum
    k_dec_full = kn * jnp.exp(g_end - g_cum)

    ri = lax.broadcasted_iota(jnp.int32, (STACK, STACK), 0)
    ci = lax.broadcasted_iota(jnp.int32, (STACK, STACK), 1)
    same = (ri // chunk) == (ci // chunk)
    causal = same & (ri >= ci)
    strict = same & (ri > ci)
    erow = lax.broadcasted_iota(jnp.int32, (STACK, groups * 64), 0)
    ecol = lax.broadcasted_iota(jnp.int32, (STACK, groups * 64), 1)
    expand_mask = (erow // chunk) == (ecol // 64)
    trow = lax.broadcasted_iota(jnp.int32, (groups * 64, STACK), 0)
    tcol = lax.broadcasted_iota(jnp.int32, (groups * 64, STACK), 1)
    expand_mask_t = (trow // 64) == (tcol // chunk)
    eye64 = (lax.broadcasted_iota(jnp.int32, (64, 64), 0)
             == lax.broadcasted_iota(jnp.int32, (64, 64), 1)).astype(BF16)

    def stack_heads(x):
        return jnp.concatenate([x[:, h * 64:(h + 1) * 64] for h in range(A_HEADS)], axis=0)

    def expand(x):
        return jnp.where(expand_mask, jnp.concatenate([x] * groups, axis=1), 0.0)

    def transpose64(x):
        return lax.dot_general(eye64, x.astype(BF16), (((1,), (1,)), ((), ())), preferred_element_type=F32)

    cs = range(nchunks)
    sls = [slice(c * 64, (c + 1) * 64) for c in cs]
    q = [stack_heads(qn[sl]) for sl in sls]
    k = [stack_heads(kn[sl]) for sl in sls]
    g_s = [stack_heads(g_cum[sl]) for sl in sls]
    g_c = [jnp.concatenate([x] * A_HEADS, axis=1) for x in g_s]
    g_r = [x.T for x in g_c]
    decay = [jnp.exp(jnp.where(causal, g_c[c] - g_r[c], NEG)) for c in cs]
    akq = [_dot_nt(jnp.concatenate([stack_heads(k_beta[sls[c]]), q[c]], axis=0), k[c]) for c in cs]
    amat = [jnp.where(strict, akq[c][:STACK] * decay[c], 0.0) for c in cs]
    qk = [akq[c][STACK:] * decay[c] for c in cs]

    def compress(x):
        return functools.reduce(lambda u, w: u + w, [x[g * chunk:(g + 1) * chunk] for g in range(groups)])

    def block_diag(xc):
        return jnp.where(same, jnp.concatenate([xc] * groups, axis=0), 0.0)

    eye_c = (lax.broadcasted_iota(jnp.int32, (chunk, STACK), 0)
             == lax.broadcasted_iota(jnp.int32, (chunk, STACK), 1) % chunk).astype(F32)
    x_inv = [eye_c - compress(jnp.where((ri // 2) == (ci // 2), amat[c], 0.0)) for c in cs]
    blk = 2
    while blk < chunk:
        merge = ((ri // (2 * blk)) == (ci // (2 * blk))) & ((ri // blk) != (ci // blk))
        left = [_dot(x_inv[c], jnp.where(merge, amat[c], 0.0)) for c in cs]
        if side_work:
            side_work.pop(0)()
        x_inv = [x_inv[c] - _dot(left[c], block_diag(x_inv[c])) for c in cs]
        blk *= 2
    while side_work:
        side_work.pop(0)()
    sol = [_dot(block_diag(x_inv[c]),
                jnp.concatenate([stack_heads(rhs_v[sls[c]]), stack_heads(rhs_k[sls[c]])], axis=1)) for c in cs]
    w_val = [x[:, :64] for x in sol]
    w_key = [x[:, 64:] for x in sol]
    q_dec = [stack_heads(q_dec_full[sl]) for sl in sls]
    kd_t = [transpose64(stack_heads(k_dec_full[sl])) for sl in sls]
    kd_t_exp = [jnp.where(expand_mask_t, jnp.concatenate([x] * groups, axis=0), 0.0) for x in kd_t]
    g_end_s = [stack_heads(g_end[sl]) for sl in sls]
    s_decay = [jnp.exp(jnp.concatenate(
        [jnp.broadcast_to(x[g * chunk:g * chunk + 1], (64, 64)) for g in range(groups)], axis=0)) for x in g_end_s]

    o_chunks = []
    if nseg > 1:
        s_old = [s_scr[c] for c in cs]
        s_b = [x.astype(BF16) for x in s_old]
        u = [w_val[c] - _dot(expand(w_key[c]), s_b[c]) for c in cs]
        o_chunks = [_dot(expand(q_dec[c]), s_b[c]) + _dot(qk[c], u[c]) for c in cs]
        for c in cs:
            s_scr[c] = s_old[c] * s_decay[c] + _dot_nt(kd_t_exp[c], transpose64(u[c]))
    else:
        state = s_scr[0]
        qk_sol = [_dot(qk[c], sol[c]) for c in cs]
        lhs = [jnp.concatenate([expand(q_dec[c] - qk_sol[c][:, 64:]), _dot(kd_t_exp[c], expand(w_key[c]))], axis=0)
               for c in cs]
        s_add = [_dot(kd_t_exp[c], w_val[c]) for c in cs]
        for c in cs:
            from_state = _dot(lhs[c], state.astype(BF16))
            o_chunks.append(from_state[:STACK] + qk_sol[c][:, :64])
            state = state * s_decay[c] - from_state[STACK:] + s_add[c]
        s_scr[0] = state
    oa = [jnp.concatenate([o[h * 64:(h + 1) * 64] for h in range(A_HEADS)], axis=1) for o in o_chunks]
    oa = oa[0] if nchunks == 1 else jnp.concatenate(oa, axis=0)
    oa = oa * lax.rsqrt(_dot(oa * oa, gavg) + EPS) * onorm_ref[...] * _silu(z[:, Z_GATE:Z_B])

    mix = jnp.concatenate([oa, results["ob"], results["oc"], results["od"]], axis=1)
    x1 = x_in + _rms(_dot(mix, wout_ref[...]), gpost_ref[...])
    x1_ref[...] = x1
    q = _dot(_rms(x1, gpre_ref[...]), wq_ref[...])
    q_ref[...] = (_attend(q, kmem_ref, vmem_ref) if attend else q).astype(q_ref.dtype)

    @pl.when(t == nt - 1)
    def _store_state():
        for s in range(nseg):
            dconv_o[s] = xq_scr[s, qoff - (A_CONV - 1):qoff, :]
            bconv_o[s] = xb_scr[s, boff - (B_CONV - 1):boff, :]
            pbuf_o[s] = xc_scr[s, coff - POOL_BUF:coff, :]
            lconv_o[s] = xd_scr[s, doff - (D_CONV - 1):doff, :]
            lh_o[s] = h_scr[s]
            for h in range(A_HEADS):
                c, sc = divmod(s, seq_per_chunk)
                g = h * seq_per_chunk + sc
                ds_o[s, h] = s_scr[c, g * 64:(g + 1) * 64, :]


def _mixer(x, state_in, layer_in, state_out, layer_out, lw, *, nseg, tn, chunk, pos0, mem_kv=None):
    bsz, tlen, d = x.shape
    rows = nseg * tn
    assert rows % 64 == 0 and tlen % tn == 0 and bsz % nseg == 0
    assert tn % SUBLANES == 0
    assert (nseg == 1 and chunk == 64) or (nseg > 1 and chunk == tn and tlen == tn)
    groups = STACK // chunk
    grid = (bsz // nseg, tlen // tn)

    def state_spec(a, layer):
        blk = (None, nseg) + a.shape[2:]
        nd = a.ndim
        return pl.BlockSpec(blk, lambda b, t: (layer, b) + (0,) * (nd - 2))

    weights = (lw["norm_mix_pre"], lw["w_in"], lw["conv_qkv"], lw["hp"], lw["onorm"], lw["dw_b"], lw["brow"],
               lw["w_pw"], lw["w_pool"], lw["conv_d"], lw["w_rg"], lw["w_ig"], lw["gavg"],
               lw["w_out"], lw["norm_mix_post"], lw["norm_x_pre"], lw["w_xq"])
    mem_kv = () if mem_kv is None else tuple(mem_kv)
    mem_spec = pl.BlockSpec((None, None, N_MEM, d), lambda b, t: (layer_out, b, 0, 0))
    n_in = 1 + len(state_in) + len(weights) + len(mem_kv)
    nt = tlen // tn
    row_spec = pl.BlockSpec((rows, d), lambda b, t: (b * nt + t, 0))
    out_shape = (jax.ShapeDtypeStruct((bsz * tlen, d), F32), jax.ShapeDtypeStruct((bsz * tlen, d), BF16)) + tuple(
        jax.ShapeDtypeStruct(a.shape, a.dtype) for a in state_out)
    kern = functools.partial(_mixer_kernel, nseg=nseg, tn=tn, chunk=chunk, pos0=pos0, attend=bool(mem_kv))
    return pl.pallas_call(
        kern,
        grid=grid,
        in_specs=[pl.BlockSpec((nseg, tn, d), lambda b, t: (b, t, 0))]
        + [state_spec(a, layer_in) for a in state_in] + [_layer_spec(w, layer_out) for w in weights]
        + [mem_spec for _ in mem_kv] + [pl.BlockSpec(memory_space=pl.ANY) for _ in state_out],
        out_specs=[row_spec, row_spec] + [state_spec(a, layer_out) for a in state_out],
        out_shape=out_shape,
        input_output_aliases={n_in + j: 2 + j for j in range(len(state_out))},
        scratch_shapes=[
            pltpu.VMEM((nseg, 8 + tn + SUBLANES, A_QKV), F32),
            pltpu.VMEM((nseg, 32 + tn + SUBLANES, W_GROUP), F32),
            pltpu.VMEM((nseg, 16 + tn, W_GROUP), F32),
            pltpu.VMEM((nseg, 8 + tn + SUBLANES, W_GROUP), F32),
            pltpu.VMEM((rows // 64 if nseg > 1 else 1, groups * 64, 64), F32),
            pltpu.VMEM((nseg, 1, W_GROUP), F32),
        ],
        compiler_params=_cparams("parallel", "arbitrary"),
        name="mixer",
    )(x, *state_in, *weights, *mem_kv, *state_out)


def _attn_cache_kernel(q_ref, k_ref, v_ref, o_ref, *, nseq, tlen):
    hrows = X_HEADS * tlen
    qrow_head = lax.broadcasted_iota(jnp.int32, (hrows, N_MEM * X_HEADS), 0) // tlen
    col_head = lax.broadcasted_iota(jnp.int32, (hrows, N_MEM * X_HEADS), 1) % X_HEADS
    own = qrow_head == col_head
    seqs = range(nseq)
    qs = [q_ref[s].astype(F32) for s in seqs]
    q4 = [jnp.concatenate([x[:, h * X_HD:(h + 1) * X_HD] for h in range(X_HEADS)], axis=0) for x in qs]
    sc = [_dot_nt(q4[s], k_ref[0, s].reshape(N_MEM * X_HEADS, X_HD)) * (X_HD ** -0.5) for s in seqs]
    p = [_softmax_rows(jnp.where(own, x, NEG)) for x in sc]
    o4 = [_dot(p[s], v_ref[0, s].reshape(N_MEM * X_HEADS, X_HD)) for s in seqs]
    for s in seqs:
        o_ref[s] = jnp.concatenate([o4[s][h * tlen:(h + 1) * tlen] for h in range(X_HEADS)], axis=1)


def _xo_ffn_kernel(o_ref, x_ref, wo_ref, gxo_ref, gpre_ref, win_ref, wout_ref, gpost_ref, y_ref):
    x = x_ref[...] + _rms(_dot(o_ref[...], wo_ref[...]), gxo_ref[...])
    gu = _dot(_rms(x, gpre_ref[...]), win_ref[...])
    act = _silu(gu[:, :FFN]) * gu[:, FFN:]
    y_ref[...] = x + _rms(_dot(act, wout_ref[...]), gpost_ref[...])


def _xo_ffn_with_cache_attn_kernel(o_ref, x_ref, wo_ref, gxo_ref, gpre_ref, win_ref, wout_ref, gpost_ref,
                                   q_ref, k_ref, v_ref, y_ref, oc_ref, *, nseq, tlen):
    _attn_cache_kernel(q_ref, k_ref, v_ref, oc_ref, nseq=nseq, tlen=tlen)
    _xo_ffn_kernel(o_ref, x_ref, wo_ref, gxo_ref, gpre_ref, win_ref, wout_ref, gpost_ref, y_ref)


def _xo_ffn(o, x, lw, layer, tm, cache_attn=None):
    m, d = x.shape
    tm = min(tm, m)
    assert m % tm == 0
    steps = m // tm
    row = lambda i: (i, 0)
    weights = (lw["w_xo"], lw["norm_x_post"], lw["norm_ffn_pre"], lw["w_ffn_in"], lw["w_ffn_out"],
               lw["norm_ffn_post"])
    in_specs = [pl.BlockSpec((tm, d), row), pl.BlockSpec((tm, d), row)] + [_layer_spec(w, layer) for w in weights]
    if cache_attn is None:
        return pl.pallas_call(
            _xo_ffn_kernel,
            grid=(steps,),
            in_specs=in_specs,
            out_specs=pl.BlockSpec((tm, d), row),
            out_shape=jax.ShapeDtypeStruct((m, d), F32),
            compiler_params=_cparams("parallel"),
            name="xo_ffn",
        )(o, x, *weights)
    q, cache_k, cache_v = cache_attn
    bsz, tlen, _ = q.shape
    assert bsz % steps == 0
    nseq = bsz // steps
    cache_spec = pl.BlockSpec((1, nseq) + cache_k.shape[2:], lambda i: (layer, i, 0, 0, 0))
    seq_spec = pl.BlockSpec((nseq, tlen, d), lambda i: (i, 0, 0))
    kern = functools.partial(_xo_ffn_with_cache_attn_kernel, nseq=nseq, tlen=tlen)
    return pl.pallas_call(
        kern,
        grid=(steps,),
        in_specs=in_specs + [seq_spec, cache_spec, cache_spec],
        out_specs=[pl.BlockSpec((tm, d), row), seq_spec],
        out_shape=[jax.ShapeDtypeStruct((m, d), F32), jax.ShapeDtypeStruct((bsz, tlen, d), F32)],
        compiler_params=_cparams("parallel"),
        name="xo_ffn_cache_attn",
    )(o, x, *weights, q, cache_k, cache_v)


def _block_diag(w):
    depth, n, k, _ = w.shape
    return (w[:, :, :, None, :] * jnp.eye(n, dtype=w.dtype)[None, :, None, :, None]).reshape(depth, n * k, n * k)


def _prep_weights(p):
    w_in = p["w_in"]
    depth = w_in.shape[0]
    off_araw, off_gate = A_QKV, A_QKV + 2 * A_HEADS
    per_head = lambda a: jnp.repeat(a, 64, axis=-1)
    w_in = jnp.concatenate(
        [w_in[..., :off_araw], w_in[..., off_gate:], per_head(w_in[..., off_araw:off_araw + A_HEADS]),
         per_head(w_in[..., off_araw + A_HEADS:off_gate])], axis=-1)
    row = lambda a: a[:, None, :]
    gavg = _block_diag(jnp.full((depth, B_GROUPS, 64, 64), 1.0 / 64, F32))
    return {
        "w_in": w_in.astype(BF16),
        "norm_mix_pre": row(p["norm_mix_pre"]),
        "conv_qkv": p["conv_qkv"],
        "hp": jnp.stack([per_head(p["a_log"]), per_head(p["dt_bias"])], axis=1),
        "onorm": row(jnp.tile(p["onorm_a"], (1, A_HEADS))),
        "dw_b": p["dw_b"],
        "brow": jnp.stack([p["dwbias_b"], p["gn_gain_b"], p["gn_bias_b"], p["scale_pool"],
                           p["conv_bias_d"], p["b_rg"], p["b_ig"], p["lam_d"]], axis=1),
        "w_pw": p["w_pw_b"].astype(BF16),
        "w_pool": _block_diag(p["w_pool"]).astype(BF16),
        "conv_d": p["conv_d"],
        "w_rg": _block_diag(p["w_rg"]).astype(BF16),
        "w_ig": _block_diag(p["w_ig"]).astype(BF16),
        "gavg": gavg.astype(BF16),
        "w_out": p["w_out"].astype(BF16),
        "norm_mix_post": row(p["norm_mix_post"]),
        "norm_x_pre": row(p["norm_x_pre"]),
        "w_xq": p["w_xq"].astype(BF16),
        "w_xo": p["w_xo"].astype(BF16),
        "norm_x_post": row(p["norm_x_post"]),
        "norm_ffn_pre": row(p["norm_ffn_pre"]),
        "w_ffn_in": p["w_ffn_in"].astype(BF16),
        "w_ffn_out": p["w_ffn_out"].astype(BF16),
        "norm_ffn_post": row(p["norm_ffn_post"]),
    }


def kernel(x_prompt, x_sample, mem_prompt, state_delta, state_delta_conv, state_conf_conv, state_pool,
           state_lru_conv, state_lru_h, cache_mem_k, cache_mem_v, norm_mix_pre, norm_mix_post, w_in, conv_qkv,
           a_log, dt_bias, onorm_a, dw_b, dwbias_b, gn_gain_b, gn_bias_b, w_pw_b, w_pool, scale_pool, conv_d,
           conv_bias_d, w_rg, b_rg, w_ig, b_ig, lam_d, w_out, norm_x_pre, norm_x_post, norm_mem, w_xq, w_xkv,
           w_xo, norm_ffn_pre, norm_ffn_post, w_ffn_in, w_ffn_out):
    depth = w_in.shape[0]
    bp, tp, d = x_prompt.shape
    bs, ts, _ = x_sample.shape
    dt = x_prompt.dtype
    per_layer = dict(norm_mix_pre=norm_mix_pre, norm_mix_post=norm_mix_post, w_in=w_in, conv_qkv=conv_qkv,
                     a_log=a_log, dt_bias=dt_bias, onorm_a=onorm_a, dw_b=dw_b, dwbias_b=dwbias_b,
                     gn_gain_b=gn_gain_b, gn_bias_b=gn_bias_b, w_pw_b=w_pw_b, w_pool=w_pool,
                     scale_pool=scale_pool, conv_d=conv_d, conv_bias_d=conv_bias_d, w_rg=w_rg, b_rg=b_rg,
                     w_ig=w_ig, b_ig=b_ig, lam_d=lam_d, w_out=w_out, norm_x_pre=norm_x_pre,
                     norm_x_post=norm_x_post, w_xq=w_xq, w_xo=w_xo, norm_ffn_pre=norm_ffn_pre,
                     norm_ffn_post=norm_ffn_post, w_ffn_in=w_ffn_in, w_ffn_out=w_ffn_out)
    mem_k_p, mem_v_p, k_rows, v_rows = _memory_kv(mem_prompt, norm_mem[:, None, :], w_xkv.astype(BF16))

    def state_shapes(b):
        return ((b, A_CONV - 1, A_QKV), (b, A_HEADS, A_DK, A_DK), (b, B_CONV - 1, W_GROUP), (b, POOL_BUF, W_GROUP),
                (b, D_CONV - 1, W_GROUP), (b, 1, W_GROUP))

    zero_state = tuple(jnp.zeros((1,) + s, dt) for s in state_shapes(bp))
    sample_state = (state_delta_conv, state_delta, state_conf_conv, state_pool, state_lru_conv,
                    state_lru_h[:, :, None, :])
    st_p = tuple(jnp.zeros((depth,) + s, dt) for s in state_shapes(bp))
    st_s = tuple(jnp.zeros((depth,) + s, dt) for s in state_shapes(bs))
    xp, xs = x_prompt, x_sample
    tn_p = min(tp, 512)
    lw = _prep_weights(per_layer)
    nseg_s = min(bs, 256 // ts)
    for l in range(depth):
        x1_p, o_p, *st_p = _mixer(xp, zero_state, 0, st_p, l, lw, nseg=1, tn=tn_p, chunk=64, pos0=0,
                                  mem_kv=(k_rows, v_rows))
        x1_s, q_s, *st_s = _mixer(xs, sample_state, l, st_s, l, lw, nseg=nseg_s, tn=ts, chunk=ts, pos0=PAST_LEN)
        xp, o_s = _xo_ffn(o_p, x1_p, lw, l, 512,
                          cache_attn=(q_s.reshape(bs, ts, d), cache_mem_k, cache_mem_v))
        xs = _xo_ffn(o_s.reshape(bs * ts, d), x1_s, lw, l, 512).reshape(bs, ts, d)
        xp = xp.reshape(bp, tp, d)
    dconv_p, delta_p, conf_p, pool_p, lconv_p, lh_p = st_p
    dconv_s, delta_s, conf_s, pool_s, lconv_s, lh_s = st_s
    return (xp, xs, delta_p, delta_s, dconv_p, dconv_s, conf_p, conf_s, pool_p, pool_s,
            lconv_p, lconv_s, lh_p[:, :, 0, :], lh_s[:, :, 0, :], mem_k_p, mem_v_p)
```

```python
import functools

import jax
import jax.numpy as jnp
from jax import lax
from jax.experimental import pallas as pl
from jax.experimental.pallas import tpu as pltpu

F32 = jnp.float32
BF16 = jnp.bfloat16
EPS = 1e-6

D_MODEL = 1024
W_GROUP = 256
A_HEADS = 4
A_DK = 64
A_CONV = 4
A_QKV = 768
B_CONV = 31
B_GROUPS = 4
POOL_BUF = 15
D_CONV = 4
LRU_C = 8.0
N_MEM = 256
X_HEADS = 4
X_HD = 256
FFN = 2816
PAST_LEN = 16384

Z_QKV = 0
Z_GATE = 768
Z_B = 1024
Z_C = 1536
Z_D = 1792
Z_G = 2304
Z_BETA = 2560
Z_COLS = 2816

SUBLANES = 8
STACK = A_HEADS * 64
NEG = -1e30
VMEM_LIMIT = 56 * 1024 * 1024


def _cparams(*sem):
    return pltpu.CompilerParams(dimension_semantics=sem, vmem_limit_bytes=VMEM_LIMIT)


def _layer_spec(a, layer):
    nd = a.ndim
    return pl.BlockSpec((None,) + a.shape[1:], lambda *_: (layer,) + (0,) * (nd - 1),
                        pipeline_mode=pl.Buffered(1))


def _rms(x, g):
    return x * lax.rsqrt(jnp.mean(x * x, axis=-1, keepdims=True) + EPS) * g


def _sigmoid(x):
    return 0.5 * jnp.tanh(0.5 * x) + 0.5


def _silu(x):
    return x * _sigmoid(x)


def _softplus(x):
    return jnp.maximum(x, 0.0) + jnp.log(1.0 + jnp.exp(-jnp.abs(x)))


def _dot(a, b):
    return jnp.dot(a.astype(BF16), b.astype(BF16), preferred_element_type=F32)


def _dot_nt(a, b):
    return lax.dot_general(a.astype(BF16), b.astype(BF16), (((1,), (1,)), ((), ())),
                           preferred_element_type=F32)


def _dot_2pass(x, w):
    hi = x.astype(BF16)
    lo = (x - hi.astype(F32)).astype(BF16)
    return (jnp.dot(hi, w, preferred_element_type=F32) + jnp.dot(lo, w, preferred_element_type=F32))


def _memory_kv_kernel(mem_ref, g_ref, w_ref, k_ref, v_ref, k_rows_ref, v_rows_ref):
    kv = _dot(_rms(mem_ref[0], g_ref[...]), w_ref[...])
    k, v = kv[:, :D_MODEL], kv[:, D_MODEL:]
    k_ref[...] = k.reshape(N_MEM, X_HEADS, X_HD)
    v_ref[...] = v.reshape(N_MEM, X_HEADS, X_HD)
    k_rows_ref[...] = k.astype(k_rows_ref.dtype)
    v_rows_ref[...] = v.astype(v_rows_ref.dtype)


def _memory_kv(mem, g, w):
    bsz, n_mem, d = mem.shape
    depth = w.shape[0]
    cache_spec = pl.BlockSpec((None, None, n_mem, X_HEADS, X_HD), lambda l, b: (l, b, 0, 0, 0))
    rows_spec = pl.BlockSpec((None, None, n_mem, d), lambda l, b: (l, b, 0, 0))
    cache_shape = jax.ShapeDtypeStruct((depth, bsz, n_mem, X_HEADS, X_HD), F32)
    rows_shape = jax.ShapeDtypeStruct((depth, bsz, n_mem, d), BF16)
    return pl.pallas_call(
        _memory_kv_kernel,
        grid=(depth, bsz),
        in_specs=[pl.BlockSpec((1, n_mem, d), lambda l, b: (b, 0, 0)),
                  pl.BlockSpec((None, 1, d), lambda l, b: (l, 0, 0)),
                  pl.BlockSpec((None, d, 2 * d), lambda l, b: (l, 0, 0))],
        out_specs=[cache_spec, cache_spec, rows_spec, rows_spec],
        out_shape=[cache_shape, cache_shape, rows_shape, rows_shape],
        compiler_params=_cparams("parallel", "arbitrary"),
        name="memory_kv",
    )(mem, g, w)


def _seg_scan_sum(x, seg_len, row_in_seg):
    s = 1
    while s < seg_len:
        x = x + jnp.where(row_in_seg >= s, pltpu.roll(x, s, axis=0), 0.0)
        s *= 2
    return x


def _linear_scan(a, b, carries, seg_rows):
    rows, width = a.shape
    nblk = rows // SUBLANES
    a = a.reshape(nblk, SUBLANES, width)
    b = b.reshape(nblk, SUBLANES, width)
    sub = lax.broadcasted_iota(jnp.int32, (1, SUBLANES, 1), 1)
    s = 1
    while s < SUBLANES:
        keep = sub >= s
        a_sh = jnp.where(keep, pltpu.roll(a, s, axis=1), 1.0)
        b_sh = jnp.where(keep, pltpu.roll(b, s, axis=1), 0.0)
        b = a * b_sh + b
        a = a * a_sh
        s *= 2
    blk_per_seg = seg_rows // SUBLANES
    outs = []
    carry = None
    for g in range(nblk):
        if g % blk_per_seg == 0:
            carry = carries[g // blk_per_seg]
        hg = a[g] * carry + b[g]
        carry = hg[SUBLANES - 1:SUBLANES]
        outs.append(hg)
    return jnp.concatenate(outs, axis=0)


def _conv_segment(scr, seg, x, w_rows, width, off, tn, row_block):
    scr[seg, off:off + tn, :] = x
    base = off - (width - 1)
    outs = []
    for r0 in range(0, tn, row_block):
        rb = min(row_block, tn - r0)
        acc = None
        for res in range(SUBLANES):
            part = None
            for p in range(base, base + width):
                if p % SUBLANES != res:
                    continue
                lo = p - res + r0
                term = scr[seg, lo:lo + rb + SUBLANES, :] * w_rows[p - base]
                part = term if part is None else part + term
            if part is None:
                continue
            piece = part[res:res + rb]
            acc = piece if acc is None else acc + piece
        outs.append(acc)
    tail = scr[seg, off + tn - (width - 1):off + tn, :]
    scr[seg, off - (width - 1):off, :] = tail
    return outs[0] if len(outs) == 1 else jnp.concatenate(outs, axis=0)


def _pool_segment(scr, seg, u, off, tn):
    scr[seg, off:off + tn, :] = u
    x = scr[seg, 0:off + tn, :]
    s2 = x + pltpu.roll(x, 1, axis=0)
    s4 = s2 + pltpu.roll(s2, 2, axis=0)
    s8 = s4 + pltpu.roll(s4, 4, axis=0)
    s16 = s8[off:] + s8[off - 8:off - 8 + tn]
    lane = lax.broadcasted_iota(jnp.int32, (1, W_GROUP), 1)
    wsum = jnp.where(lane < 64, s2[off:], jnp.where(lane < 128, s4[off:], jnp.where(lane < 192, s8[off:], s16)))
    tail = scr[seg, off + tn - POOL_BUF:off + tn, :]
    scr[seg, off - POOL_BUF:off, :] = tail
    return wsum


def _softmax_rows(s):
    e = jnp.exp(s - jnp.max(s, axis=-1, keepdims=True))
    return e / jnp.sum(e, axis=-1, keepdims=True)


def _attend(q, k_ref, v_ref):
    heads = [slice(h * X_HD, (h + 1) * X_HD) for h in range(X_HEADS)]
    s = [_dot_nt(q[:, sl], k_ref[:, sl]) * (X_HD ** -0.5) for sl in heads]
    p = [_softmax_rows(x) for x in s]
    return jnp.concatenate([_dot(p[h], v_ref[:, heads[h]]) for h in range(X_HEADS)], axis=1)


def _mixer_kernel(x_ref, dconv_ref, ds_ref, bconv_ref, pbuf_ref, lconv_ref, lh_ref,
                  gin_ref, win_ref, wqkv_ref, hp_ref, onorm_ref, dw_ref, brow_ref, wpw_ref, wpool_ref, convd_ref,
                  wrg_ref, wig_ref, gavg_ref, wout_ref, gpost_ref, gpre_ref, wq_ref,
                  *rest, nseg, tn, chunk, pos0, attend):
    if attend:
        kmem_ref, vmem_ref, *rest = rest
    (x1_ref, q_ref, dconv_o, ds_o, bconv_o, pbuf_o, lconv_o, lh_o,
     xq_scr, xb_scr, xc_scr, xd_scr, s_scr, h_scr) = rest
    t = pl.program_id(1)
    nt = pl.num_programs(1)
    rows = nseg * tn
    nchunks = rows // 64
    groups = STACK // chunk
    seq_per_chunk = 64 // chunk
    qoff, boff, coff, doff = 8, 32, 16, 8

    @pl.when(t == 0)
    def _load_state():
        for s in range(nseg):
            xq_scr[s, 0:qoff, :] = jnp.zeros((qoff, A_QKV), F32)
            xq_scr[s, qoff + tn:, :] = jnp.zeros((SUBLANES, A_QKV), F32)
            xb_scr[s, 0:SUBLANES, :] = jnp.zeros((SUBLANES, W_GROUP), F32)
            xb_scr[s, boff + tn:, :] = jnp.zeros((SUBLANES, W_GROUP), F32)
            xc_scr[s, 0:SUBLANES, :] = jnp.zeros((SUBLANES, W_GROUP), F32)
            xd_scr[s, 0:doff, :] = jnp.zeros((doff, W_GROUP), F32)
            xd_scr[s, doff + tn:, :] = jnp.zeros((SUBLANES, W_GROUP), F32)
            xq_scr[s, qoff - (A_CONV - 1):qoff, :] = dconv_ref[s]
            xb_scr[s, boff - (B_CONV - 1):boff, :] = bconv_ref[s]
            xc_scr[s, coff - POOL_BUF:coff, :] = pbuf_ref[s]
            xd_scr[s, doff - (D_CONV - 1):doff, :] = lconv_ref[s]
            h_scr[s] = lh_ref[s]
            for h in range(A_HEADS):
                c, sc = divmod(s, seq_per_chunk)
                g = h * seq_per_chunk + sc
                s_scr[c, g * 64:(g + 1) * 64, :] = ds_ref[s, h]

    x_in = x_ref[0] if nseg == 1 else x_ref[...].reshape(rows, D_MODEL)
    z = _dot(_rms(x_in, gin_ref[...]), win_ref[...])

    def per_segment(fn, x):
        outs = [fn(s, x[s * tn:(s + 1) * tn]) for s in range(nseg)]
        return outs[0] if nseg == 1 else jnp.concatenate(outs, axis=0)

    row = lax.broadcasted_iota(jnp.int32, (rows, 1), 0)
    row_in_seg = row % tn
    brow = brow_ref[...]
    gavg = gavg_ref[...]
    results = {}

    def section_b_conv():
        ub = z[:, Z_B:Z_C]
        glu = ub[:, :W_GROUP] * _sigmoid(ub[:, W_GROUP:])
        dw_rows = [dw_ref[j:j + 1, :] for j in range(B_CONV)]
        yb = per_segment(lambda s, x: _conv_segment(xb_scr, s, x, dw_rows, B_CONV, boff, tn, 64), glu)
        results["yb"] = yb + brow[0:1]

    def section_b_norm():
        yb = results["yb"]
        yc = yb - _dot_2pass(yb, gavg)
        var = _dot(yc * yc, gavg)
        yn = yc * lax.rsqrt(var + EPS) * brow[1:2] + brow[2:3]
        results["ob"] = _dot(_silu(yn), wpw_ref[...])

    def section_c():
        uc = z[:, Z_C:Z_D]
        wsum = per_segment(lambda s, x: _pool_segment(xc_scr, s, x, coff, tn), uc)
        lane = lax.broadcasted_iota(jnp.int32, (1, W_GROUP), 1)
        wlen = jnp.where(lane < 64, 2.0, jnp.where(lane < 128, 4.0, jnp.where(lane < 192, 8.0, 16.0)))
        pos = (pos0 + t * tn + row_in_seg).astype(F32)
        cnt = jnp.minimum(pos + 1.0, wlen)
        results["oc"] = _dot(wsum / cnt - uc, wpool_ref[...]) * brow[3:4]

    def section_d_gates():
        ud = z[:, Z_D:Z_G]
        results["gate_d"] = jax.nn.gelu(ud[:, :W_GROUP])
        cd_rows = [convd_ref[j:j + 1, :] for j in range(D_CONV)]
        xr = per_segment(lambda s, x: _conv_segment(xd_scr, s, x, cd_rows, D_CONV, doff, tn, 64), ud[:, W_GROUP:])
        xr = xr + brow[4:5]
        r_gate = _sigmoid(_dot(xr, wrg_ref[...]) + brow[5:6])
        i_gate = _sigmoid(_dot(xr, wig_ref[...]) + brow[6:7])
        log_a = -LRU_C * r_gate * _softplus(-brow[7:8])
        a = jnp.exp(log_a)
        results["lru_a"] = a
        results["lru_b"] = jnp.sqrt(-jnp.tanh(log_a) * (a * a + 1.0)) * (i_gate * xr)

    def section_d_scan():
        hseq = _linear_scan(results["lru_a"], results["lru_b"], [h_scr[s] for s in range(nseg)], tn)
        results["od"] = results["gate_d"] * hseq
        for s in range(nseg):
            h_scr[s] = hseq[(s + 1) * tn - 1:(s + 1) * tn, :]

    side_work = [section_b_conv, section_b_norm, section_c, section_d_gates, section_d_scan]

    wq_rows = [wqkv_ref[j:j + 1, :] for j in range(A_CONV)]
    yq = per_segment(lambda s, x: _conv_segment(xq_scr, s, x, wq_rows, A_CONV, qoff, tn, 64), z[:, Z_QKV:Z_GATE])
    yq = _silu(yq)
    hp = hp_ref[...]
    g_log = -jnp.exp(hp[0:1]) * _softplus(z[:, Z_G:Z_BETA] + hp[1:2])
    beta = _sigmoid(z[:, Z_BETA:Z_COLS])
    g_cum = _seg_scan_sum(g_log, chunk, row % chunk)
    g_end = jnp.concatenate(
        [jnp.broadcast_to(g_cum[(i + 1) * chunk - 1:(i + 1) * chunk], (chunk, W_GROUP)) for i in range(rows // chunk)],
        axis=0)
    gsum = gavg * 64.0
    qn, kn, vv = yq[:, 0:256], yq[:, 256:512], yq[:, 512:768]
    qn = qn * lax.rsqrt(_dot(qn * qn, gsum) + EPS) * (A_DK ** -0.5)
    kn = kn * lax.rsqrt(_dot(kn * kn, gsum) + EPS)
    e_cum = jnp.exp(g_cum)
    rhs_v = vv * beta
    k_beta = kn * beta
    rhs_k = k_beta * e_cum
    q_dec_full = qn * e_cum
    k_dec_full = kn * jnp.exp(g_end - g_cum)

    ri = lax.broadcasted_iota(jnp.int32, (STACK, STACK), 0)
    ci = lax.broadcasted_iota(jnp.int32, (STACK, STACK), 1)
    same = (ri // chunk) == (ci // chunk)
    causal = same & (ri >= ci)
    strict = same & (ri > ci)
    erow = lax.broadcasted_iota(jnp.int32, (STACK, groups * 64), 0)
    ecol = lax.broadcasted_iota(jnp.int32, (STACK, groups * 64), 1)
    expand_mask = (erow // chunk) == (ecol // 64)
    trow = lax.broadcasted_iota(jnp.int32, (groups * 64, STACK), 0)
    tcol = lax.broadcasted_iota(jnp.int32, (groups * 64, STACK), 1)
    expand_mask_t = (trow // 64) == (tcol // chunk)
    eye64 = (lax.broadcasted_iota(jnp.int32, (64, 64), 0)
             == lax.broadcasted_iota(jnp.int32, (64, 64), 1)).astype(BF16)

    def stack_heads(x):
        return jnp.concatenate([x[:, h * 64:(h + 1) * 64] for h in range(A_HEADS)], axis=0)

    def expand(x):
        return jnp.where(expand_mask, jnp.concatenate([x] * groups, axis=1), 0.0)

    def transpose64(x):
        return lax.dot_general(eye64, x.astype(BF16), (((1,), (1,)), ((), ())), preferred_element_type=F32)

    cs = range(nchunks)
    sls = [slice(c * 64, (c + 1) * 64) for c in cs]
    q = [stack_heads(qn[sl]) for sl in sls]
    k = [stack_heads(kn[sl]) for sl in sls]
    g_s = [stack_heads(g_cum[sl]) for sl in sls]
    g_c = [jnp.concatenate([x] * A_HEADS, axis=1) for x in g_s]
    g_r = [x.T for x in g_c]
    decay = [jnp.exp(jnp.where(causal, g_c[c] - g_r[c], NEG)) for c in cs]
    akq = [_dot_nt(jnp.concatenate([stack_heads(k_beta[sls[c]]), q[c]], axis=0), k[c]) for c in cs]
    amat = [jnp.where(strict, akq[c][:STACK] * decay[c], 0.0) for c in cs]
    qk = [akq[c][STACK:] * decay[c] for c in cs]

    def compress(x):
        return functools.reduce(lambda u, w: u + w, [x[g * chunk:(g + 1) * chunk] for g in range(groups)])

    def block_diag(xc):
        return jnp.where(same, jnp.concatenate([xc] * groups, axis=0), 0.0)

    eye_c = (lax.broadcasted_iota(jnp.int32, (chunk, STACK), 0)
             == lax.broadcasted_iota(jnp.int32, (chunk, STACK), 1) % chunk).astype(F32)
    x_inv = [eye_c - compress(jnp.where((ri // 2) == (ci // 2), amat[c], 0.0)) for c in cs]
    blk = 2
    while blk < chunk:
        merge = ((ri // (2 * blk)) == (ci // (2 * blk))) & ((ri // blk) != (ci // blk))
        left = [_dot(x_inv[c], jnp.where(merge, amat[c], 0.0)) for c in cs]
        if side_work:
            side_work.pop(0)()
        x_inv = [x_inv[c] - _dot(left[c], block_diag(x_inv[c])) for c in cs]
        blk *= 2
    while side_work:
        side_work.pop(0)()
    sol = [_dot(block_diag(x_inv[c]),
                jnp.concatenate([stack_heads(rhs_v[sls[c]]), stack_heads(rhs_k[sls[c]])], axis=1)) for c in cs]
    w_val = [x[:, :64] for x in sol]
    w_key = [x[:, 64:] for x in sol]
    q_dec = [stack_heads(q_dec_full[sl]) for sl in sls]
    kd_t = [transpose64(stack_heads(k_dec_full[sl])) for sl in sls]
    kd_t_exp = [jnp.where(expand_mask_t, jnp.concatenate([x] * groups, axis=0), 0.0) for x in kd_t]
    g_end_s = [stack_heads(g_end[sl]) for sl in sls]
    s_decay = [jnp.exp(jnp.concatenate(
        [jnp.broadcast_to(x[g * chunk:g * chunk + 1], (64, 64)) for g in range(groups)], axis=0)) for x in g_end_s]

    o_chunks = []
    if nseg > 1:
        s_old = [s_scr[c] for c in cs]
        s_b = [x.astype(BF16) for x in s_old]
        u = [w_val[c] - _dot(expand(w_key[c]), s_b[c]) for c in cs]
        o_chunks = [_dot(expand(q_dec[c]), s_b[c]) + _dot(qk[c], u[c]) for c in cs]
        for c in cs:
            s_scr[c] = s_old[c] * s_decay[c] + _dot_nt(kd_t_exp[c], transpose64(u[c]))
    else:
        state = s_scr[0]
        qk_sol = [_dot(qk[c], sol[c]) for c in cs]
        lhs = [jnp.concatenate([expand(q_dec[c] - qk_sol[c][:, 64:]), _dot(kd_t_exp[c], expand(w_key[c]))], axis=0)
               for c in cs]
        s_add = [_dot(kd_t_exp[c], w_val[c]) for c in cs]
        for c in cs:
            from_state = _dot(lhs[c], state.astype(BF16))
            o_chunks.append(from_state[:STACK] + qk_sol[c][:, :64])
            state = state * s_decay[c] - from_state[STACK:] + s_add[c]
        s_scr[0] = state
    oa = [jnp.concatenate([o[h * 64:(h + 1) * 64] for h in range(A_HEADS)], axis=1) for o in o_chunks]
    oa = oa[0] if nchunks == 1 else jnp.concatenate(oa, axis=0)
    oa = oa * lax.rsqrt(_dot(oa * oa, gavg) + EPS) * onorm_ref[...] * _silu(z[:, Z_GATE:Z_B])

    mix = jnp.concatenate([oa, results["ob"], results["oc"], results["od"]], axis=1)
    x1 = x_in + _rms(_dot(mix, wout_ref[...]), gpost_ref[...])
    x1_ref[...] = x1
    q = _dot(_rms(x1, gpre_ref[...]), wq_ref[...])
    q_ref[...] = (_attend(q, kmem_ref, vmem_ref) if attend else q).astype(q_ref.dtype)

    @pl.when(t == nt - 1)
    def _store_state():
        for s in range(nseg):
            dconv_o[s] = xq_scr[s, qoff - (A_CONV - 1):qoff, :]
            bconv_o[s] = xb_scr[s, boff - (B_CONV - 1):boff, :]
            pbuf_o[s] = xc_scr[s, coff - POOL_BUF:coff, :]
            lconv_o[s] = xd_scr[s, doff - (D_CONV - 1):doff, :]
            lh_o[s] = h_scr[s]
            for h in range(A_HEADS):
                c, sc = divmod(s, seq_per_chunk)
                g = h * seq_per_chunk + sc
                ds_o[s, h] = s_scr[c, g * 64:(g + 1) * 64, :]


def _mixer(x, state, layer, lw, *, nseg, tn, chunk, pos0, mem_kv=None):
    bsz, tlen, d = x.shape
    rows = nseg * tn
    assert rows % 64 == 0 and tlen % tn == 0 and bsz % nseg == 0
    assert tn % SUBLANES == 0
    assert (nseg == 1 and chunk == 64) or (nseg > 1 and chunk == tn and tlen == tn)
    groups = STACK // chunk
    grid = (bsz // nseg, tlen // tn)

    def state_spec(a, layer):
        blk = (None, nseg) + a.shape[2:]
        nd = a.ndim
        return pl.BlockSpec(blk, lambda b, t: (layer, b) + (0,) * (nd - 2))

    weights = (lw["norm_mix_pre"], lw["w_in"], lw["conv_qkv"], lw["hp"], lw["onorm"], lw["dw_b"], lw["brow"],
               lw["w_pw"], lw["w_pool"], lw["conv_d"], lw["w_rg"], lw["w_ig"], lw["gavg"],
               lw["w_out"], lw["norm_mix_post"], lw["norm_x_pre"], lw["w_xq"])
    mem_kv = () if mem_kv is None else tuple(mem_kv)
    mem_spec = pl.BlockSpec((None, None, N_MEM, d), lambda b, t: (layer, b, 0, 0))
    nt = tlen // tn
    row_spec = pl.BlockSpec((rows, d), lambda b, t: (b * nt + t, 0))
    out_shape = (jax.ShapeDtypeStruct((bsz * tlen, d), F32), jax.ShapeDtypeStruct((bsz * tlen, d), BF16)) + tuple(
        jax.ShapeDtypeStruct(a.shape, a.dtype) for a in state)
    kern = functools.partial(_mixer_kernel, nseg=nseg, tn=tn, chunk=chunk, pos0=pos0, attend=bool(mem_kv))
    return pl.pallas_call(
        kern,
        grid=grid,
        in_specs=[pl.BlockSpec((nseg, tn, d), lambda b, t: (b, t, 0))]
        + [state_spec(a, layer) for a in state] + [_layer_spec(w, layer) for w in weights]
        + [mem_spec for _ in mem_kv],
        out_specs=[row_spec, row_spec] + [state_spec(a, layer) for a in state],
        out_shape=out_shape,
        input_output_aliases={1 + j: 2 + j for j in range(len(state))},
        scratch_shapes=[
            pltpu.VMEM((nseg, 8 + tn + SUBLANES, A_QKV), F32),
            pltpu.VMEM((nseg, 32 + tn + SUBLANES, W_GROUP), F32),
            pltpu.VMEM((nseg, 16 + tn, W_GROUP), F32),
            pltpu.VMEM((nseg, 8 + tn + SUBLANES, W_GROUP), F32),
            pltpu.VMEM((rows // 64 if nseg > 1 else 1, groups * 64, 64), F32),
            pltpu.VMEM((nseg, 1, W_GROUP), F32),
        ],
        compiler_params=_cparams("parallel", "arbitrary"),
        name="mixer",
    )(x, *state, *weights, *mem_kv)


def _attn_cache_kernel(q_ref, k_ref, v_ref, o_ref, *, nseq, tlen):
    hrows = X_HEADS * tlen
    qrow_head = lax.broadcasted_iota(jnp.int32, (hrows, N_MEM * X_HEADS), 0) // tlen
    col_head = lax.broadcasted_iota(jnp.int32, (hrows, N_MEM * X_HEADS), 1) % X_HEADS
    own = qrow_head == col_head
    seqs = range(nseq)
    qs = [q_ref[s].astype(F32) for s in seqs]
    q4 = [jnp.concatenate([x[:, h * X_HD:(h + 1) * X_HD] for h in range(X_HEADS)], axis=0) for x in qs]
    sc = [_dot_nt(q4[s], k_ref[0, s].reshape(N_MEM * X_HEADS, X_HD)) * (X_HD ** -0.5) for s in seqs]
    p = [_softmax_rows(jnp.where(own, x, NEG)) for x in sc]
    o4 = [_dot(p[s], v_ref[0, s].reshape(N_MEM * X_HEADS, X_HD)) for s in seqs]
    for s in seqs:
        o_ref[s] = jnp.concatenate([o4[s][h * tlen:(h + 1) * tlen] for h in range(X_HEADS)], axis=1)


def _xo_ffn_kernel(o_ref, x_ref, wo_ref, gxo_ref, gpre_ref, win_ref, wout_ref, gpost_ref, y_ref):
    x = x_ref[...] + _rms(_dot(o_ref[...], wo_ref[...]), gxo_ref[...])
    gu = _dot(_rms(x, gpre_ref[...]), win_ref[...])
    act = _silu(gu[:, :FFN]) * gu[:, FFN:]
    y_ref[...] = x + _rms(_dot(act, wout_ref[...]), gpost_ref[...])


def _xo_ffn_with_cache_attn_kernel(o_ref, x_ref, wo_ref, gxo_ref, gpre_ref, win_ref, wout_ref, gpost_ref,
                                   q_ref, k_ref, v_ref, y_ref, oc_ref, *, nseq, tlen):
    _attn_cache_kernel(q_ref, k_ref, v_ref, oc_ref, nseq=nseq, tlen=tlen)
    _xo_ffn_kernel(o_ref, x_ref, wo_ref, gxo_ref, gpre_ref, win_ref, wout_ref, gpost_ref, y_ref)


def _xo_ffn(o, x, lw, layer, tm, cache_attn=None):
    m, d = x.shape
    tm = min(tm, m)
    assert m % tm == 0
    steps = m // tm
    row = lambda i: (i, 0)
    weights = (lw["w_xo"], lw["norm_x_post"], lw["norm_ffn_pre"], lw["w_ffn_in"], lw["w_ffn_out"],
               lw["norm_ffn_post"])
    in_specs = [pl.BlockSpec((tm, d), row), pl.BlockSpec((tm, d), row)] + [_layer_spec(w, layer) for w in weights]
    if cache_attn is None:
        return pl.pallas_call(
            _xo_ffn_kernel,
            grid=(steps,),
            in_specs=in_specs,
            out_specs=pl.BlockSpec((tm, d), row),
            out_shape=jax.ShapeDtypeStruct((m, d), F32),
            compiler_params=_cparams("parallel"),
            name="xo_ffn",
        )(o, x, *weights)
    q, cache_k, cache_v = cache_attn
    bsz, tlen, _ = q.shape
    assert bsz % steps == 0
    nseq = bsz // steps
    cache_spec = pl.BlockSpec((1, nseq) + cache_k.shape[2:], lambda i: (layer, i, 0, 0, 0))
    seq_spec = pl.BlockSpec((nseq, tlen, d), lambda i: (i, 0, 0))
    kern = functools.partial(_xo_ffn_with_cache_attn_kernel, nseq=nseq, tlen=tlen)
    return pl.pallas_call(
        kern,
        grid=(steps,),
        in_specs=in_specs + [seq_spec, cache_spec, cache_spec],
        out_specs=[pl.BlockSpec((tm, d), row), seq_spec],
        out_shape=[jax.ShapeDtypeStruct((m, d), F32), jax.ShapeDtypeStruct((bsz, tlen, d), F32)],
        compiler_params=_cparams("parallel"),
        name="xo_ffn_cache_attn",
    )(o, x, *weights, q, cache_k, cache_v)


def _block_diag(w):
    depth, n, k, _ = w.shape
    return (w[:, :, :, None, :] * jnp.eye(n, dtype=w.dtype)[None, :, None, :, None]).reshape(depth, n * k, n * k)


def _prep_weights(p):
    w_in = p["w_in"]
    depth = w_in.shape[0]
    off_araw, off_gate = A_QKV, A_QKV + 2 * A_HEADS
    per_head = lambda a: jnp.repeat(a, 64, axis=-1)
    w_in = jnp.concatenate(
        [w_in[..., :off_araw], w_in[..., off_gate:], per_head(w_in[..., off_araw:off_araw + A_HEADS]),
         per_head(w_in[..., off_araw + A_HEADS:off_gate])], axis=-1)
    row = lambda a: a[:, None, :]
    gavg = _block_diag(jnp.full((depth, B_GROUPS, 64, 64), 1.0 / 64, F32))
    return {
        "w_in": w_in.astype(BF16),
        "norm_mix_pre": row(p["norm_mix_pre"]),
        "conv_qkv": p["conv_qkv"],
        "hp": jnp.stack([per_head(p["a_log"]), per_head(p["dt_bias"])], axis=1),
        "onorm": row(jnp.tile(p["onorm_a"], (1, A_HEADS))),
        "dw_b": p["dw_b"],
        "brow": jnp.stack([p["dwbias_b"], p["gn_gain_b"], p["gn_bias_b"], p["scale_pool"],
                           p["conv_bias_d"], p["b_rg"], p["b_ig"], p["lam_d"]], axis=1),
        "w_pw": p["w_pw_b"].astype(BF16),
        "w_pool": _block_diag(p["w_pool"]).astype(BF16),
        "conv_d": p["conv_d"],
        "w_rg": _block_diag(p["w_rg"]).astype(BF16),
        "w_ig": _block_diag(p["w_ig"]).astype(BF16),
        "gavg": gavg.astype(BF16),
        "w_out": p["w_out"].astype(BF16),
        "norm_mix_post": row(p["norm_mix_post"]),
        "norm_x_pre": row(p["norm_x_pre"]),
        "w_xq": p["w_xq"].astype(BF16),
        "w_xo": p["w_xo"].astype(BF16),
        "norm_x_post": row(p["norm_x_post"]),
        "norm_ffn_pre": row(p["norm_ffn_pre"]),
        "w_ffn_in": p["w_ffn_in"].astype(BF16),
        "w_ffn_out": p["w_ffn_out"].astype(BF16),
        "norm_ffn_post": row(p["norm_ffn_post"]),
    }


def kernel(x_prompt, x_sample, mem_prompt, state_delta, state_delta_conv, state_conf_conv, state_pool,
           state_lru_conv, state_lru_h, cache_mem_k, cache_mem_v, norm_mix_pre, norm_mix_post, w_in, conv_qkv,
           a_log, dt_bias, onorm_a, dw_b, dwbias_b, gn_gain_b, gn_bias_b, w_pw_b, w_pool, scale_pool, conv_d,
           conv_bias_d, w_rg, b_rg, w_ig, b_ig, lam_d, w_out, norm_x_pre, norm_x_post, norm_mem, w_xq, w_xkv,
           w_xo, norm_ffn_pre, norm_ffn_post, w_ffn_in, w_ffn_out):
    depth = w_in.shape[0]
    bp, tp, d = x_prompt.shape
    bs, ts, _ = x_sample.shape
    dt = x_prompt.dtype
    per_layer = dict(norm_mix_pre=norm_mix_pre, norm_mix_post=norm_mix_post, w_in=w_in, conv_qkv=conv_qkv,
                     a_log=a_log, dt_bias=dt_bias, onorm_a=onorm_a, dw_b=dw_b, dwbias_b=dwbias_b,
                     gn_gain_b=gn_gain_b, gn_bias_b=gn_bias_b, w_pw_b=w_pw_b, w_pool=w_pool,
                     scale_pool=scale_pool, conv_d=conv_d, conv_bias_d=conv_bias_d, w_rg=w_rg, b_rg=b_rg,
                     w_ig=w_ig, b_ig=b_ig, lam_d=lam_d, w_out=w_out, norm_x_pre=norm_x_pre,
                     norm_x_post=norm_x_post, w_xq=w_xq, w_xo=w_xo, norm_ffn_pre=norm_ffn_pre,
                     norm_ffn_post=norm_ffn_post, w_ffn_in=w_ffn_in, w_ffn_out=w_ffn_out)
    mem_k_p, mem_v_p, k_rows, v_rows = _memory_kv(mem_prompt, norm_mem[:, None, :], w_xkv.astype(BF16))

    def state_shapes(b):
        return ((b, A_CONV - 1, A_QKV), (b, A_HEADS, A_DK, A_DK), (b, B_CONV - 1, W_GROUP), (b, POOL_BUF, W_GROUP),
                (b, D_CONV - 1, W_GROUP), (b, 1, W_GROUP))

    st_p = tuple(jnp.zeros((depth,) + s, dt) for s in state_shapes(bp))
    st_s = (state_delta_conv, state_delta, state_conf_conv, state_pool, state_lru_conv, state_lru_h[:, :, None, :])
    xp, xs = x_prompt, x_sample
    tn_p = min(tp, 512)
    lw = _prep_weights(per_layer)
    nseg_s = min(bs, 256 // ts)
    for l in range(depth):
        x1_p, o_p, *st_p = _mixer(xp, st_p, l, lw, nseg=1, tn=tn_p, chunk=64, pos0=0, mem_kv=(k_rows, v_rows))
        x1_s, q_s, *st_s = _mixer(xs, st_s, l, lw, nseg=nseg_s, tn=ts, chunk=ts, pos0=PAST_LEN)
        xp, o_s = _xo_ffn(o_p, x1_p, lw, l, 512,
                          cache_attn=(q_s.reshape(bs, ts, d), cache_mem_k, cache_mem_v))
        xs = _xo_ffn(o_s.reshape(bs * ts, d), x1_s, lw, l, 512).reshape(bs, ts, d)
        xp = xp.reshape(bp, tp, d)
    dconv_p, delta_p, conf_p, pool_p, lconv_p, lh_p = st_p
    dconv_s, delta_s, conf_s, pool_s, lconv_s, lh_s = st_s
    return (xp, xs, delta_p, delta_s, dconv_p, dconv_s, conf_p, conf_s, pool_p, pool_s,
            lconv_p, lconv_s, lh_p[:, :, 0, :], lh_s[:, :, 0, :], mem_k_p, mem_v_p)
```

```python
import functools

import jax
import jax.numpy as jnp
from jax import lax
from jax.experimental import pallas as pl
from jax.experimental.pallas import tpu as pltpu

F32 = jnp.float32
BF16 = jnp.bfloat16
EPS = 1e-6

D_MODEL = 1024
W_GROUP = 256
A_HEADS = 4
A_DK = 64
A_CONV = 4
A_QKV = 768
B_CONV = 31
B_GROUPS = 4
POOL_BUF = 15
D_CONV = 4
LRU_C = 8.0
N_MEM = 256
X_HEADS = 4
X_HD = 256
FFN = 2816
PAST_LEN = 16384

Z_QKV = 0
Z_GATE = 768
Z_B = 1024
Z_C = 1536
Z_D = 1792
Z_G = 2304
Z_BETA = 2560
Z_COLS = 2816

FFN_CHUNK = 256
SUBLANES = 8
STACK = A_HEADS * 64
NEG = -1e30
VMEM_LIMIT = 56 * 1024 * 1024


def _cparams(*sem):
    return pltpu.CompilerParams(dimension_semantics=sem, vmem_limit_bytes=VMEM_LIMIT)


def _layer_spec(a, layer):
    nd = a.ndim
    return pl.BlockSpec((None,) + a.shape[1:], lambda *_: (layer,) + (0,) * (nd - 1),
                        pipeline_mode=pl.Buffered(1))


def _rms(x, g):
    return x * lax.rsqrt(jnp.mean(x * x, axis=-1, keepdims=True) + EPS) * g


def _sigmoid(x):
    return 0.5 * jnp.tanh(0.5 * x) + 0.5


def _silu(x):
    return x * _sigmoid(x)


def _softplus(x):
    return jnp.maximum(x, 0.0) + jnp.log(1.0 + jnp.exp(-jnp.abs(x)))


def _dot(a, b):
    return jnp.dot(a.astype(BF16), b.astype(BF16), preferred_element_type=F32)


def _dot_nt(a, b):
    return lax.dot_general(a.astype(BF16), b.astype(BF16), (((1,), (1,)), ((), ())),
                           preferred_element_type=F32)


def _dot_2pass(x, w):
    hi = x.astype(BF16)
    lo = (x - hi.astype(F32)).astype(BF16)
    return (jnp.dot(hi, w, preferred_element_type=F32) + jnp.dot(lo, w, preferred_element_type=F32))


def _memory_kv_kernel(mem_ref, g_ref, w_ref, k_ref, v_ref, k_rows_ref, v_rows_ref):
    kv = _dot(_rms(mem_ref[0], g_ref[...]), w_ref[...])
    k, v = kv[:, :D_MODEL], kv[:, D_MODEL:]
    k_ref[...] = k.reshape(N_MEM, X_HEADS, X_HD)
    v_ref[...] = v.reshape(N_MEM, X_HEADS, X_HD)
    k_rows_ref[...] = k.astype(k_rows_ref.dtype)
    v_rows_ref[...] = v.astype(v_rows_ref.dtype)


def _memory_kv(mem, g, w):
    bsz, n_mem, d = mem.shape
    depth = w.shape[0]
    cache_spec = pl.BlockSpec((None, None, n_mem, X_HEADS, X_HD), lambda l, b: (l, b, 0, 0, 0))
    rows_spec = pl.BlockSpec((None, None, n_mem, d), lambda l, b: (l, b, 0, 0))
    cache_shape = jax.ShapeDtypeStruct((depth, bsz, n_mem, X_HEADS, X_HD), F32)
    rows_shape = jax.ShapeDtypeStruct((depth, bsz, n_mem, d), BF16)
    return pl.pallas_call(
        _memory_kv_kernel,
        grid=(depth, bsz),
        in_specs=[pl.BlockSpec((1, n_mem, d), lambda l, b: (b, 0, 0)),
                  pl.BlockSpec((None, 1, d), lambda l, b: (l, 0, 0)),
                  pl.BlockSpec((None, d, 2 * d), lambda l, b: (l, 0, 0))],
        out_specs=[cache_spec, cache_spec, rows_spec, rows_spec],
        out_shape=[cache_shape, cache_shape, rows_shape, rows_shape],
        compiler_params=_cparams("parallel", "arbitrary"),
        name="memory_kv",
    )(mem, g, w)


def _seg_scan_sum(x, seg_len, row_in_seg):
    s = 1
    while s < seg_len:
        x = x + jnp.where(row_in_seg >= s, pltpu.roll(x, s, axis=0), 0.0)
        s *= 2
    return x


def _linear_scan(a, b, carries, seg_rows):
    rows, width = a.shape
    nblk = rows // SUBLANES
    a = a.reshape(nblk, SUBLANES, width)
    b = b.reshape(nblk, SUBLANES, width)
    sub = lax.broadcasted_iota(jnp.int32, (1, SUBLANES, 1), 1)
    s = 1
    while s < SUBLANES:
        keep = sub >= s
        a_sh = jnp.where(keep, pltpu.roll(a, s, axis=1), 1.0)
        b_sh = jnp.where(keep, pltpu.roll(b, s, axis=1), 0.0)
        b = a * b_sh + b
        a = a * a_sh
        s *= 2
    blk_per_seg = seg_rows // SUBLANES
    outs = []
    carry = None
    for g in range(nblk):
        if g % blk_per_seg == 0:
            carry = carries[g // blk_per_seg]
        hg = a[g] * carry + b[g]
        carry = hg[SUBLANES - 1:SUBLANES]
        outs.append(hg)
    return jnp.concatenate(outs, axis=0)


def _conv_segment(scr, seg, x, w_rows, width, off, tn, row_block):
    scr[seg, off:off + tn, :] = x
    base = off - (width - 1)
    outs = []
    for r0 in range(0, tn, row_block):
        rb = min(row_block, tn - r0)
        acc = None
        for res in range(SUBLANES):
            part = None
            for p in range(base, base + width):
                if p % SUBLANES != res:
                    continue
                lo = p - res + r0
                term = scr[seg, lo:lo + rb + SUBLANES, :] * w_rows[p - base]
                part = term if part is None else part + term
            if part is None:
                continue
            piece = part[res:res + rb]
            acc = piece if acc is None else acc + piece
        outs.append(acc)
    tail = scr[seg, off + tn - (width - 1):off + tn, :]
    scr[seg, off - (width - 1):off, :] = tail
    return outs[0] if len(outs) == 1 else jnp.concatenate(outs, axis=0)


def _pool_segment(scr, seg, u, off, tn):
    scr[seg, off:off + tn, :] = u
    x = scr[seg, 0:off + tn, :]
    s2 = x + pltpu.roll(x, 1, axis=0)
    s4 = s2 + pltpu.roll(s2, 2, axis=0)
    s8 = s4 + pltpu.roll(s4, 4, axis=0)
    s16 = s8[off:] + s8[off - 8:off - 8 + tn]
    lane = lax.broadcasted_iota(jnp.int32, (1, W_GROUP), 1)
    wsum = jnp.where(lane < 64, s2[off:], jnp.where(lane < 128, s4[off:], jnp.where(lane < 192, s8[off:], s16)))
    tail = scr[seg, off + tn - POOL_BUF:off + tn, :]
    scr[seg, off - POOL_BUF:off, :] = tail
    return wsum


def _softmax_rows(s):
    e = jnp.exp(s - jnp.max(s, axis=-1, keepdims=True))
    return e / jnp.sum(e, axis=-1, keepdims=True)


def _attend(q, k_ref, v_ref):
    heads = [slice(h * X_HD, (h + 1) * X_HD) for h in range(X_HEADS)]
    s = [_dot_nt(q[:, sl], k_ref[:, sl]) * (X_HD ** -0.5) for sl in heads]
    p = [_softmax_rows(x) for x in s]
    return jnp.concatenate([_dot(p[h], v_ref[:, heads[h]]) for h in range(X_HEADS)], axis=1)


def _mixer_kernel(x_ref, dconv_ref, ds_ref, bconv_ref, pbuf_ref, lconv_ref, lh_ref,
                  gin_ref, win_ref, wqkv_ref, hp_ref, onorm_ref, dw_ref, brow_ref, wpw_ref, wpool_ref, convd_ref,
                  wrg_ref, wig_ref, gavg_ref, wout_ref, gpost_ref, gpre_ref, wq_ref,
                  *rest, nseg, tn, chunk, pos0, attend):
    if attend:
        kmem_ref, vmem_ref, *rest = rest
    (x1_ref, q_ref, dconv_o, ds_o, bconv_o, pbuf_o, lconv_o, lh_o,
     xq_scr, xb_scr, xc_scr, xd_scr, s_scr, h_scr) = rest
    t = pl.program_id(1)
    nt = pl.num_programs(1)
    rows = nseg * tn
    nchunks = rows // 64
    groups = STACK // chunk
    seq_per_chunk = 64 // chunk
    qoff, boff, coff, doff = 8, 32, 16, 8

    @pl.when(t == 0)
    def _load_state():
        for s in range(nseg):
            xq_scr[s, 0:qoff, :] = jnp.zeros((qoff, A_QKV), F32)
            xq_scr[s, qoff + tn:, :] = jnp.zeros((SUBLANES, A_QKV), F32)
            xb_scr[s, 0:SUBLANES, :] = jnp.zeros((SUBLANES, W_GROUP), F32)
            xb_scr[s, boff + tn:, :] = jnp.zeros((SUBLANES, W_GROUP), F32)
            xc_scr[s, 0:SUBLANES, :] = jnp.zeros((SUBLANES, W_GROUP), F32)
            xd_scr[s, 0:doff, :] = jnp.zeros((doff, W_GROUP), F32)
            xd_scr[s, doff + tn:, :] = jnp.zeros((SUBLANES, W_GROUP), F32)
            xq_scr[s, qoff - (A_CONV - 1):qoff, :] = dconv_ref[s]
            xb_scr[s, boff - (B_CONV - 1):boff, :] = bconv_ref[s]
            xc_scr[s, coff - POOL_BUF:coff, :] = pbuf_ref[s]
            xd_scr[s, doff - (D_CONV - 1):doff, :] = lconv_ref[s]
            h_scr[s] = lh_ref[s]
            for h in range(A_HEADS):
                c, sc = divmod(s, seq_per_chunk)
                g = h * seq_per_chunk + sc
                s_scr[c, g * 64:(g + 1) * 64, :] = ds_ref[s, h]

    x_in = x_ref[0] if nseg == 1 else x_ref[...].reshape(rows, D_MODEL)
    z = _dot(_rms(x_in, gin_ref[...]), win_ref[...])

    def per_segment(fn, x):
        outs = [fn(s, x[s * tn:(s + 1) * tn]) for s in range(nseg)]
        return outs[0] if nseg == 1 else jnp.concatenate(outs, axis=0)

    row = lax.broadcasted_iota(jnp.int32, (rows, 1), 0)
    row_in_seg = row % tn
    brow = brow_ref[...]
    gavg = gavg_ref[...]
    results = {}

    def section_b_conv():
        ub = z[:, Z_B:Z_C]
        glu = ub[:, :W_GROUP] * _sigmoid(ub[:, W_GROUP:])
        dw_rows = [dw_ref[j:j + 1, :] for j in range(B_CONV)]
        yb = per_segment(lambda s, x: _conv_segment(xb_scr, s, x, dw_rows, B_CONV, boff, tn, 64), glu)
        results["yb"] = yb + brow[0:1]

    def section_b_norm():
        yb = results["yb"]
        yc = yb - _dot_2pass(yb, gavg)
        var = _dot(yc * yc, gavg)
        yn = yc * lax.rsqrt(var + EPS) * brow[1:2] + brow[2:3]
        results["ob"] = _dot(_silu(yn), wpw_ref[...])

    def section_c():
        uc = z[:, Z_C:Z_D]
        wsum = per_segment(lambda s, x: _pool_segment(xc_scr, s, x, coff, tn), uc)
        lane = lax.broadcasted_iota(jnp.int32, (1, W_GROUP), 1)
        wlen = jnp.where(lane < 64, 2.0, jnp.where(lane < 128, 4.0, jnp.where(lane < 192, 8.0, 16.0)))
        pos = (pos0 + t * tn + row_in_seg).astype(F32)
        cnt = jnp.minimum(pos + 1.0, wlen)
        results["oc"] = _dot(wsum / cnt - uc, wpool_ref[...]) * brow[3:4]

    def section_d_gates():
        ud = z[:, Z_D:Z_G]
        results["gate_d"] = jax.nn.gelu(ud[:, :W_GROUP])
        cd_rows = [convd_ref[j:j + 1, :] for j in range(D_CONV)]
        xr = per_segment(lambda s, x: _conv_segment(xd_scr, s, x, cd_rows, D_CONV, doff, tn, 64), ud[:, W_GROUP:])
        xr = xr + brow[4:5]
        r_gate = _sigmoid(_dot(xr, wrg_ref[...]) + brow[5:6])
        i_gate = _sigmoid(_dot(xr, wig_ref[...]) + brow[6:7])
        log_a = -LRU_C * r_gate * _softplus(-brow[7:8])
        a = jnp.exp(log_a)
        results["lru_a"] = a
        results["lru_b"] = jnp.sqrt(-jnp.tanh(log_a) * (a * a + 1.0)) * (i_gate * xr)

    def section_d_scan():
        hseq = _linear_scan(results["lru_a"], results["lru_b"], [h_scr[s] for s in range(nseg)], tn)
        results["od"] = results["gate_d"] * hseq
        for s in range(nseg):
            h_scr[s] = hseq[(s + 1) * tn - 1:(s + 1) * tn, :]

    side_work = [section_b_conv, section_b_norm, section_c, section_d_gates, section_d_scan]

    wq_rows = [wqkv_ref[j:j + 1, :] for j in range(A_CONV)]
    yq = per_segment(lambda s, x: _conv_segment(xq_scr, s, x, wq_rows, A_CONV, qoff, tn, 64), z[:, Z_QKV:Z_GATE])
    yq = _silu(yq)
    hp = hp_ref[...]
    g_log = -jnp.exp(hp[0:1]) * _softplus(z[:, Z_G:Z_BETA] + hp[1:2])
    beta = _sigmoid(z[:, Z_BETA:Z_COLS])
    g_cum = _seg_scan_sum(g_log, chunk, row % chunk)
    g_end = jnp.concatenate(
        [jnp.broadcast_to(g_cum[(i + 1) * chunk - 1:(i + 1) * chunk], (chunk, W_GROUP)) for i in range(rows // chunk)],
        axis=0)
    gsum = gavg * 64.0
    qn, kn, vv = yq[:, 0:256], yq[:, 256:512], yq[:, 512:768]
    qn = qn * lax.rsqrt(_dot(qn * qn, gsum) + EPS) * (A_DK ** -0.5)
    kn = kn * lax.rsqrt(_dot(kn * kn, gsum) + EPS)
    e_cum = jnp.exp(g_cum)
    rhs_v = vv * beta
    k_beta = kn * beta
    rhs_k = k_beta * e_cum
    q_dec_full = qn * e_cum
    k_dec_full = kn * jnp.exp(g_end - g_cum)

    ri = lax.broadcasted_iota(jnp.int32, (STACK, STACK), 0)
    ci = lax.broadcasted_iota(jnp.int32, (STACK, STACK), 1)
    same = (ri // chunk) == (ci // chunk)
    causal = same & (ri >= ci)
    strict = same & (ri > ci)
    erow = lax.broadcasted_iota(jnp.int32, (STACK, groups * 64), 0)
    ecol = lax.broadcasted_iota(jnp.int32, (STACK, groups * 64), 1)
    expand_mask = (erow // chunk) == (ecol // 64)
    trow = lax.broadcasted_iota(jnp.int32, (groups * 64, STACK), 0)
    tcol = lax.broadcasted_iota(jnp.int32, (groups * 64, STACK), 1)
    expand_mask_t = (trow // 64) == (tcol // chunk)
    eye64 = (lax.broadcasted_iota(jnp.int32, (64, 64), 0)
             == lax.broadcasted_iota(jnp.int32, (64, 64), 1)).astype(BF16)

    def stack_heads(x):
        return jnp.concatenate([x[:, h * 64:(h + 1) * 64] for h in range(A_HEADS)], axis=0)

    def expand(x):
        return jnp.where(expand_mask, jnp.concatenate([x] * groups, axis=1), 0.0)

    def transpose64(x):
        return lax.dot_general(eye64, x.astype(BF16), (((1,), (1,)), ((), ())), preferred_element_type=F32)

    cs = range(nchunks)
    sls = [slice(c * 64, (c + 1) * 64) for c in cs]
    q = [stack_heads(qn[sl]) for sl in sls]
    k = [stack_heads(kn[sl]) for sl in sls]
    g_s = [stack_heads(g_cum[sl]) for sl in sls]
    g_c = [jnp.concatenate([x] * A_HEADS, axis=1) for x in g_s]
    g_r = [x.T for x in g_c]
    decay = [jnp.exp(jnp.where(causal, g_c[c] - g_r[c], NEG)) for c in cs]
    akq = [_dot_nt(jnp.concatenate([stack_heads(k_beta[sls[c]]), q[c]], axis=0), k[c]) for c in cs]
    amat = [jnp.where(strict, akq[c][:STACK] * decay[c], 0.0) for c in cs]
    qk = [akq[c][STACK:] * decay[c] for c in cs]

    def compress(x):
        return functools.reduce(lambda u, w: u + w, [x[g * chunk:(g + 1) * chunk] for g in range(groups)])

    def block_diag(xc):
        return jnp.where(same, jnp.concatenate([xc] * groups, axis=0), 0.0)

    eye_c = (lax.broadcasted_iota(jnp.int32, (chunk, STACK), 0)
             == lax.broadcasted_iota(jnp.int32, (chunk, STACK), 1) % chunk).astype(F32)
    x_inv = [eye_c - compress(jnp.where((ri // 2) == (ci // 2), amat[c], 0.0)) for c in cs]
    blk = 2
    while blk < chunk:
        merge = ((ri // (2 * blk)) == (ci // (2 * blk))) & ((ri // blk) != (ci // blk))
        left = [_dot(x_inv[c], jnp.where(merge, amat[c], 0.0)) for c in cs]
        if side_work:
            side_work.pop(0)()
        x_inv = [x_inv[c] - _dot(left[c], block_diag(x_inv[c])) for c in cs]
        blk *= 2
    while side_work:
        side_work.pop(0)()
    sol = [_dot(block_diag(x_inv[c]),
                jnp.concatenate([stack_heads(rhs_v[sls[c]]), stack_heads(rhs_k[sls[c]])], axis=1)) for c in cs]
    w_val = [x[:, :64] for x in sol]
    w_key = [x[:, 64:] for x in sol]
    q_dec = [stack_heads(q_dec_full[sl]) for sl in sls]
    kd_t = [transpose64(stack_heads(k_dec_full[sl])) for sl in sls]
    kd_t_exp = [jnp.where(expand_mask_t, jnp.concatenate([x] * groups, axis=0), 0.0) for x in kd_t]
    g_end_s = [stack_heads(g_end[sl]) for sl in sls]
    s_decay = [jnp.exp(jnp.concatenate(
        [jnp.broadcast_to(x[g * chunk:g * chunk + 1], (64, 64)) for g in range(groups)], axis=0)) for x in g_end_s]

    o_chunks = []
    if nseg > 1:
        s_old = [s_scr[c] for c in cs]
        s_b = [x.astype(BF16) for x in s_old]
        u = [w_val[c] - _dot(expand(w_key[c]), s_b[c]) for c in cs]
        o_chunks = [_dot(expand(q_dec[c]), s_b[c]) + _dot(qk[c], u[c]) for c in cs]
        for c in cs:
            s_scr[c] = s_old[c] * s_decay[c] + _dot_nt(kd_t_exp[c], transpose64(u[c]))
    else:
        state = s_scr[0]
        qk_sol = [_dot(qk[c], sol[c]) for c in cs]
        lhs = [jnp.concatenate([expand(q_dec[c] - qk_sol[c][:, 64:]), _dot(kd_t_exp[c], expand(w_key[c]))], axis=0)
               for c in cs]
        s_add = [_dot(kd_t_exp[c], w_val[c]) for c in cs]
        for c in cs:
            from_state = _dot(lhs[c], state.astype(BF16))
            o_chunks.append(from_state[:STACK] + qk_sol[c][:, :64])
            state = state * s_decay[c] - from_state[STACK:] + s_add[c]
        s_scr[0] = state
    oa = [jnp.concatenate([o[h * 64:(h + 1) * 64] for h in range(A_HEADS)], axis=1) for o in o_chunks]
    oa = oa[0] if nchunks == 1 else jnp.concatenate(oa, axis=0)
    oa = oa * lax.rsqrt(_dot(oa * oa, gavg) + EPS) * onorm_ref[...] * _silu(z[:, Z_GATE:Z_B])

    mix = jnp.concatenate([oa, results["ob"], results["oc"], results["od"]], axis=1)
    x1 = x_in + _rms(_dot(mix, wout_ref[...]), gpost_ref[...])
    x1_ref[...] = x1
    q = _dot(_rms(x1, gpre_ref[...]), wq_ref[...])
    q_ref[...] = (_attend(q, kmem_ref, vmem_ref) if attend else q).astype(q_ref.dtype)

    @pl.when(t == nt - 1)
    def _store_state():
        for s in range(nseg):
            dconv_o[s] = xq_scr[s, qoff - (A_CONV - 1):qoff, :]
            bconv_o[s] = xb_scr[s, boff - (B_CONV - 1):boff, :]
            pbuf_o[s] = xc_scr[s, coff - POOL_BUF:coff, :]
            lconv_o[s] = xd_scr[s, doff - (D_CONV - 1):doff, :]
            lh_o[s] = h_scr[s]
            for h in range(A_HEADS):
                c, sc = divmod(s, seq_per_chunk)
                g = h * seq_per_chunk + sc
                ds_o[s, h] = s_scr[c, g * 64:(g + 1) * 64, :]


def _mixer(x, state, layer, lw, *, nseg, tn, chunk, pos0, mem_kv=None):
    bsz, tlen, d = x.shape
    rows = nseg * tn
    assert rows % 64 == 0 and tlen % tn == 0 and bsz % nseg == 0
    assert tn % SUBLANES == 0
    assert (nseg == 1 and chunk == 64) or (nseg > 1 and chunk == tn and tlen == tn)
    groups = STACK // chunk
    grid = (bsz // nseg, tlen // tn)

    def state_spec(a, layer):
        blk = (None, nseg) + a.shape[2:]
        nd = a.ndim
        return pl.BlockSpec(blk, lambda b, t: (layer, b) + (0,) * (nd - 2))

    weights = (lw["norm_mix_pre"], lw["w_in"], lw["conv_qkv"], lw["hp"], lw["onorm"], lw["dw_b"], lw["brow"],
               lw["w_pw"], lw["w_pool"], lw["conv_d"], lw["w_rg"], lw["w_ig"], lw["gavg"],
               lw["w_out"], lw["norm_mix_post"], lw["norm_x_pre"], lw["w_xq"])
    mem_kv = () if mem_kv is None else tuple(mem_kv)
    mem_spec = pl.BlockSpec((None, None, N_MEM, d), lambda b, t: (layer, b, 0, 0))
    nt = tlen // tn
    row_spec = pl.BlockSpec((rows, d), lambda b, t: (b * nt + t, 0))
    out_shape = (jax.ShapeDtypeStruct((bsz * tlen, d), F32), jax.ShapeDtypeStruct((bsz * tlen, d), BF16)) + tuple(
        jax.ShapeDtypeStruct(a.shape, a.dtype) for a in state)
    kern = functools.partial(_mixer_kernel, nseg=nseg, tn=tn, chunk=chunk, pos0=pos0, attend=bool(mem_kv))
    return pl.pallas_call(
        kern,
        grid=grid,
        in_specs=[pl.BlockSpec((nseg, tn, d), lambda b, t: (b, t, 0))]
        + [state_spec(a, layer) for a in state] + [_layer_spec(w, layer) for w in weights]
        + [mem_spec for _ in mem_kv],
        out_specs=[row_spec, row_spec] + [state_spec(a, layer) for a in state],
        out_shape=out_shape,
        input_output_aliases={1 + j: 2 + j for j in range(len(state))},
        scratch_shapes=[
            pltpu.VMEM((nseg, 8 + tn + SUBLANES, A_QKV), F32),
            pltpu.VMEM((nseg, 32 + tn + SUBLANES, W_GROUP), F32),
            pltpu.VMEM((nseg, 16 + tn, W_GROUP), F32),
            pltpu.VMEM((nseg, 8 + tn + SUBLANES, W_GROUP), F32),
            pltpu.VMEM((rows // 64 if nseg > 1 else 1, groups * 64, 64), F32),
            pltpu.VMEM((nseg, 1, W_GROUP), F32),
        ],
        compiler_params=_cparams("parallel", "arbitrary"),
        name="mixer",
    )(x, *state, *weights, *mem_kv)


def _attn_cache_stages(q_ref, k_ref, v_ref, o_ref, *, nseq, tlen):
    hrows = X_HEADS * tlen
    qrow_head = lax.broadcasted_iota(jnp.int32, (hrows, N_MEM * X_HEADS), 0) // tlen
    col_head = lax.broadcasted_iota(jnp.int32, (hrows, N_MEM * X_HEADS), 1) % X_HEADS
    own = qrow_head == col_head
    seqs = range(nseq)
    st = {}

    def scores():
        qs = [q_ref[s].astype(F32) for s in seqs]
        q4 = [jnp.concatenate([x[:, h * X_HD:(h + 1) * X_HD] for h in range(X_HEADS)], axis=0) for x in qs]
        st["sc"] = [_dot_nt(q4[s], k_ref[0, s].reshape(N_MEM * X_HEADS, X_HD)) * (X_HD ** -0.5) for s in seqs]

    def softmax():
        st["p"] = [_softmax_rows(jnp.where(own, x, NEG)) for x in st["sc"]]

    def values():
        o4 = [_dot(st["p"][s], v_ref[0, s].reshape(N_MEM * X_HEADS, X_HD)) for s in seqs]
        for s in seqs:
            o_ref[s] = jnp.concatenate([o4[s][h * tlen:(h + 1) * tlen] for h in range(X_HEADS)], axis=1)

    return scores, softmax, values


def _xo_ffn_kernel(o_ref, x_ref, wo_ref, gxo_ref, gpre_ref, win_ref, wout_ref, gpost_ref, y_ref, side_stages=()):
    x = x_ref[...] + _rms(_dot(o_ref[...], wo_ref[...]), gxo_ref[...])
    h_in = _rms(x, gpre_ref[...]).astype(BF16)
    pending = list(side_stages)
    acc = None
    for n, j in enumerate(range(0, FFN, FFN_CHUNK)):
        gate = jnp.dot(h_in, win_ref[:, j:j + FFN_CHUNK], preferred_element_type=F32)
        up = jnp.dot(h_in, win_ref[:, FFN + j:FFN + j + FFN_CHUNK], preferred_element_type=F32)
        part = _dot(_silu(gate) * up, wout_ref[j:j + FFN_CHUNK, :])
        acc = part if acc is None else acc + part
        if pending and n % 3 == 1:
            pending.pop(0)()
    while pending:
        pending.pop(0)()
    y_ref[...] = x + _rms(acc, gpost_ref[...])


def _xo_ffn_with_cache_attn_kernel(o_ref, x_ref, wo_ref, gxo_ref, gpre_ref, win_ref, wout_ref, gpost_ref,
                                   q_ref, k_ref, v_ref, y_ref, oc_ref, *, nseq, tlen):
    stages = _attn_cache_stages(q_ref, k_ref, v_ref, oc_ref, nseq=nseq, tlen=tlen)
    _xo_ffn_kernel(o_ref, x_ref, wo_ref, gxo_ref, gpre_ref, win_ref, wout_ref, gpost_ref, y_ref, side_stages=stages)


def _xo_ffn(o, x, lw, layer, tm, cache_attn=None):
    m, d = x.shape
    tm = min(tm, m)
    assert m % tm == 0
    steps = m // tm
    row = lambda i: (i, 0)
    weights = (lw["w_xo"], lw["norm_x_post"], lw["norm_ffn_pre"], lw["w_ffn_in"], lw["w_ffn_out"],
               lw["norm_ffn_post"])
    in_specs = [pl.BlockSpec((tm, d), row), pl.BlockSpec((tm, d), row)] + [_layer_spec(w, layer) for w in weights]
    if cache_attn is None:
        return pl.pallas_call(
            _xo_ffn_kernel,
            grid=(steps,),
            in_specs=in_specs,
            out_specs=pl.BlockSpec((tm, d), row),
            out_shape=jax.ShapeDtypeStruct((m, d), F32),
            compiler_params=_cparams("parallel"),
            name="xo_ffn",
        )(o, x, *weights)
    q, cache_k, cache_v = cache_attn
    bsz, tlen, _ = q.shape
    assert bsz % steps == 0
    nseq = bsz // steps
    cache_spec = pl.BlockSpec((1, nseq) + cache_k.shape[2:], lambda i: (layer, i, 0, 0, 0))
    seq_spec = pl.BlockSpec((nseq, tlen, d), lambda i: (i, 0, 0))
    kern = functools.partial(_xo_ffn_with_cache_attn_kernel, nseq=nseq, tlen=tlen)
    return pl.pallas_call(
        kern,
        grid=(steps,),
        in_specs=in_specs + [seq_spec, cache_spec, cache_spec],
        out_specs=[pl.BlockSpec((tm, d), row), seq_spec],
        out_shape=[jax.ShapeDtypeStruct((m, d), F32), jax.ShapeDtypeStruct((bsz, tlen, d), F32)],
        compiler_params=_cparams("parallel"),
        name="xo_ffn_cache_attn",
    )(o, x, *weights, q, cache_k, cache_v)


def _block_diag(w):
    depth, n, k, _ = w.shape
    return (w[:, :, :, None, :] * jnp.eye(n, dtype=w.dtype)[None, :, None, :, None]).reshape(depth, n * k, n * k)


def _prep_weights(p):
    w_in = p["w_in"]
    depth = w_in.shape[0]
    off_araw, off_gate = A_QKV, A_QKV + 2 * A_HEADS
    per_head = lambda a: jnp.repeat(a, 64, axis=-1)
    w_in = jnp.concatenate(
        [w_in[..., :off_araw], w_in[..., off_gate:], per_head(w_in[..., off_araw:off_araw + A_HEADS]),
         per_head(w_in[..., off_araw + A_HEADS:off_gate])], axis=-1)
    row = lambda a: a[:, None, :]
    gavg = _block_diag(jnp.full((depth, B_GROUPS, 64, 64), 1.0 / 64, F32))
    return {
        "w_in": w_in.astype(BF16),
        "norm_mix_pre": row(p["norm_mix_pre"]),
        "conv_qkv": p["conv_qkv"],
        "hp": jnp.stack([per_head(p["a_log"]), per_head(p["dt_bias"])], axis=1),
        "onorm": row(jnp.tile(p["onorm_a"], (1, A_HEADS))),
        "dw_b": p["dw_b"],
        "brow": jnp.stack([p["dwbias_b"], p["gn_gain_b"], p["gn_bias_b"], p["scale_pool"],
                           p["conv_bias_d"], p["b_rg"], p["b_ig"], p["lam_d"]], axis=1),
        "w_pw": p["w_pw_b"].astype(BF16),
        "w_pool": _block_diag(p["w_pool"]).astype(BF16),
        "conv_d": p["conv_d"],
        "w_rg": _block_diag(p["w_rg"]).astype(BF16),
        "w_ig": _block_diag(p["w_ig"]).astype(BF16),
        "gavg": gavg.astype(BF16),
        "w_out": p["w_out"].astype(BF16),
        "norm_mix_post": row(p["norm_mix_post"]),
        "norm_x_pre": row(p["norm_x_pre"]),
        "w_xq": p["w_xq"].astype(BF16),
        "w_xo": p["w_xo"].astype(BF16),
        "norm_x_post": row(p["norm_x_post"]),
        "norm_ffn_pre": row(p["norm_ffn_pre"]),
        "w_ffn_in": p["w_ffn_in"].astype(BF16),
        "w_ffn_out": p["w_ffn_out"].astype(BF16),
        "norm_ffn_post": row(p["norm_ffn_post"]),
    }


def kernel(x_prompt, x_sample, mem_prompt, state_delta, state_delta_conv, state_conf_conv, state_pool,
           state_lru_conv, state_lru_h, cache_mem_k, cache_mem_v, norm_mix_pre, norm_mix_post, w_in, conv_qkv,
           a_log, dt_bias, onorm_a, dw_b, dwbias_b, gn_gain_b, gn_bias_b, w_pw_b, w_pool, scale_pool, conv_d,
           conv_bias_d, w_rg, b_rg, w_ig, b_ig, lam_d, w_out, norm_x_pre, norm_x_post, norm_mem, w_xq, w_xkv,
           w_xo, norm_ffn_pre, norm_ffn_post, w_ffn_in, w_ffn_out):
    depth = w_in.shape[0]
    bp, tp, d = x_prompt.shape
    bs, ts, _ = x_sample.shape
    dt = x_prompt.dtype
    per_layer = dict(norm_mix_pre=norm_mix_pre, norm_mix_post=norm_mix_post, w_in=w_in, conv_qkv=conv_qkv,
                     a_log=a_log, dt_bias=dt_bias, onorm_a=onorm_a, dw_b=dw_b, dwbias_b=dwbias_b,
                     gn_gain_b=gn_gain_b, gn_bias_b=gn_bias_b, w_pw_b=w_pw_b, w_pool=w_pool,
                     scale_pool=scale_pool, conv_d=conv_d, conv_bias_d=conv_bias_d, w_rg=w_rg, b_rg=b_rg,
                     w_ig=w_ig, b_ig=b_ig, lam_d=lam_d, w_out=w_out, norm_x_pre=norm_x_pre,
                     norm_x_post=norm_x_post, w_xq=w_xq, w_xo=w_xo, norm_ffn_pre=norm_ffn_pre,
                     norm_ffn_post=norm_ffn_post, w_ffn_in=w_ffn_in, w_ffn_out=w_ffn_out)
    mem_k_p, mem_v_p, k_rows, v_rows = _memory_kv(mem_prompt, norm_mem[:, None, :], w_xkv.astype(BF16))

    def state_shapes(b):
        return ((b, A_CONV - 1, A_QKV), (b, A_HEADS, A_DK, A_DK), (b, B_CONV - 1, W_GROUP), (b, POOL_BUF, W_GROUP),
                (b, D_CONV - 1, W_GROUP), (b, 1, W_GROUP))

    st_p = tuple(jnp.zeros((depth,) + s, dt) for s in state_shapes(bp))
    st_s = (state_delta_conv, state_delta, state_conf_conv, state_pool, state_lru_conv, state_lru_h[:, :, None, :])
    xp, xs = x_prompt, x_sample
    tn_p = min(tp, 512)
    lw = _prep_weights(per_layer)
    nseg_s = min(bs, 256 // ts)
    for l in range(depth):
        x1_p, o_p, *st_p = _mixer(xp, st_p, l, lw, nseg=1, tn=tn_p, chunk=64, pos0=0, mem_kv=(k_rows, v_rows))
        x1_s, q_s, *st_s = _mixer(xs, st_s, l, lw, nseg=nseg_s, tn=ts, chunk=ts, pos0=PAST_LEN)
        xp, o_s = _xo_ffn(o_p, x1_p, lw, l, 512,
                          cache_attn=(q_s.reshape(bs, ts, d), cache_mem_k, cache_mem_v))
        xs = _xo_ffn(o_s.reshape(bs * ts, d), x1_s, lw, l, 512).reshape(bs, ts, d)
        xp = xp.reshape(bp, tp, d)
    dconv_p, delta_p, conf_p, pool_p, lconv_p, lh_p = st_p
    dconv_s, delta_s, conf_s, pool_s, lconv_s, lh_s = st_s
    return (xp, xs, delta_p, delta_s, dconv_p, dconv_s, conf_p, conf_s, pool_p, pool_s,
            lconv_p, lconv_s, lh_p[:, :, 0, :], lh_s[:, :, 0, :], mem_k_p, mem_v_p)
```
